```python
import jax, jax.numpy as jnp
from jax import lax
import numpy as np

D_MODEL = 1024
BATCH = 4
SEQ = 4096
DEPTH = 1
DEC_BATCH = 8
DEC_SEQ = 8192
PAST_LEN = 128

GRID_W = 64
D_FOURIER = D_MODEL // 2
N_FGROUPS = 4
FG_DIM = D_FOURIER // N_FGROUPS
D_NA = D_MODEL - D_FOURIER
N_HEADS = 8
HEAD_DIM = D_NA // N_HEADS
MAX_WIN_ROWS = 8
WIN_COLS = 16
RPB_ROWS = 2 * MAX_WIN_ROWS - 1
RPB_COLS = 2 * WIN_COLS - 1
D_IN = D_FOURIER + 3 * D_NA
D_FF = -(-8 * D_MODEL // (3 * 256)) * 256
N_MOD = 6
EPS = 1e-6

kernel_name = "hymba_fnet_natten_encoder"


def rmsnorm(x, g):
    xf = x.astype(jnp.float32)
    y = xf * lax.rsqrt(jnp.mean(xf * xf, axis=-1, keepdims=True) + EPS)
    return (y * g.astype(jnp.float32)).astype(x.dtype)


def fourier_mix(u, w_fmix):
    b, t, _ = u.shape
    ug = u.reshape(b, t, N_FGROUPS, FG_DIM).astype(jnp.float32)
    f = jnp.fft.fft2(ug, axes=(1, 3), norm="ortho").real.astype(u.dtype)
    out = jnp.einsum('btgc,gcd->btgd', f, w_fmix)
    return out.reshape(b, t, D_FOURIER)


def neighbourhood_attention(q, k, v, rpb):
    b, t, h, dh = q.shape
    rows = t // GRID_W
    kh = min(MAX_WIN_ROWS, rows)
    q5 = q.reshape(b, rows, GRID_W, h, dh)
    k5 = k.reshape(b, rows, GRID_W, h, dh)
    v5 = v.reshape(b, rows, GRID_W, h, dh)
    cols = np.arange(GRID_W)
    col_start = np.clip(cols - WIN_COLS // 2, 0, GRID_W - WIN_COLS)
    col_idx = col_start[:, None] + np.arange(WIN_COLS)[None, :]
    dc = col_idx - cols[:, None] + (WIN_COLS - 1)
    scale = HEAD_DIM ** -0.5

    def row_step(args):
        qi, i = args
        rs = jnp.clip(i - kh // 2, 0, rows - kh)
        kwin = lax.dynamic_slice_in_dim(k5, rs, kh, axis=1)
        vwin = lax.dynamic_slice_in_dim(v5, rs, kh, axis=1)
        kc = kwin[:, :, col_idx]
        vc = vwin[:, :, col_idx]
        dr = rs + jnp.arange(kh) - i + (MAX_WIN_ROWS - 1)
        bias = rpb[:, dr[None, :, None], dc[:, None, :]]
        bias = jnp.transpose(bias, (1, 0, 2, 3)).astype(jnp.float32)
        s = jnp.einsum('bqhd,brqkhd->bqhrk', qi, kc,
                       preferred_element_type=jnp.float32) * scale + bias[None]
        p = jax.nn.softmax(s.reshape(b, GRID_W, h, kh * WIN_COLS), axis=-1)
        p = p.reshape(b, GRID_W, h, kh, WIN_COLS).astype(v.dtype)
        return jnp.einsum('bqhrk,brqkhd->bqhd', p, vc)

    out = lax.map(row_step, (jnp.transpose(q5, (1, 0, 2, 3, 4)), jnp.arange(rows)))
    return jnp.transpose(out, (1, 0, 2, 3, 4)).reshape(b, t, h * dh)


def encoder_layer(x, c, w_ada, b_ada, g_attn, w_in, g_q, g_k, w_fmix, rpb,
                  g_fout, g_aout, w_o, g_ffn, w_gate, w_up, w_down):
    b, t, _ = x.shape
    mod = (jax.nn.silu(c) @ w_ada + b_ada)[:, None, :]
    shift1, scale1, gate1, shift2, scale2, gate2 = jnp.split(mod, N_MOD, axis=-1)

    h = rmsnorm(x, g_attn) * (1 + scale1) + shift1
    z = h @ w_in
    u, q, k, v = jnp.split(z, [D_FOURIER, D_FOURIER + D_NA, D_FOURIER + 2 * D_NA], axis=-1)
    q = rmsnorm(q.reshape(b, t, N_HEADS, HEAD_DIM), g_q)
    k = rmsnorm(k.reshape(b, t, N_HEADS, HEAD_DIM), g_k)
    v = v.reshape(b, t, N_HEADS, HEAD_DIM)
    f_out = fourier_mix(u, w_fmix)
    a_out = neighbourhood_attention(q, k, v, rpb)
    mix = jnp.concatenate([rmsnorm(f_out, g_fout), rmsnorm(a_out, g_aout)], axis=-1) @ w_o
    x = x + gate1 * mix

    h2 = rmsnorm(x, g_ffn) * (1 + scale2) + shift2
    ff = (jax.nn.silu(h2 @ w_gate) * (h2 @ w_up)) @ w_down
    return x + gate2 * ff


def setup_inputs(seed: int = 0) -> dict:
    key = jax.random.key(seed)
    ks = jax.random.split(key, 20)
    f32 = jnp.float32
    nrm = lambda k, shape, s: jax.random.normal(k, shape, f32) * s
    gain = lambda k, shape: 1.0 + 0.01 * jax.random.normal(k, shape, f32)
    L = DEPTH
    return {
        "x_prompt": nrm(ks[0], (BATCH, SEQ, D_MODEL), 1.0),
        "x_sample": nrm(ks[1], (DEC_BATCH, DEC_SEQ, D_MODEL), 1.0),
        "c_prompt": nrm(ks[2], (BATCH, D_MODEL), 1.0),
        "c_sample": nrm(ks[3], (DEC_BATCH, D_MODEL), 1.0),
        "w_ada": nrm(ks[4], (L, D_MODEL, N_MOD * D_MODEL), D_MODEL ** -0.5),
        "b_ada": nrm(ks[5], (L, N_MOD * D_MODEL), 0.01),
        "g_attn": gain(ks[6], (L, D_MODEL)),
        "w_in": nrm(ks[7], (L, D_MODEL, D_IN), D_MODEL ** -0.5),
        "g_q": gain(ks[8], (L, HEAD_DIM)),
        "g_k": gain(ks[9], (L, HEAD_DIM)),
        "w_fmix": nrm(ks[10], (L, N_FGROUPS, FG_DIM, FG_DIM), FG_DIM ** -0.5),
        "rpb": nrm(ks[11], (L, N_HEADS, RPB_ROWS, RPB_COLS), 0.02),
        "g_fout": gain(ks[12], (L, D_FOURIER)),
        "g_aout": gain(ks[13], (L, D_NA)),
        "w_o": nrm(ks[14], (L, D_MODEL, D_MODEL), D_MODEL ** -0.5),
        "g_ffn": gain(ks[15], (L, D_MODEL)),
        "w_gate": nrm(ks[16], (L, D_MODEL, D_FF), D_MODEL ** -0.5),
        "w_up": nrm(ks[17], (L, D_MODEL, D_FF), D_MODEL ** -0.5),
        "w_down": nrm(ks[18], (L, D_FF, D_MODEL), D_FF ** -0.5),
    }


def reference(x_prompt, x_sample, c_prompt, c_sample, w_ada, b_ada, g_attn, w_in, g_q, g_k,
              w_fmix, rpb, g_fout, g_aout, w_o, g_ffn, w_gate, w_up, w_down):
    def trunk(x, c):
        for l in range(DEPTH):
            x = encoder_layer(x, c, w_ada[l], b_ada[l], g_attn[l], w_in[l], g_q[l], g_k[l],
                              w_fmix[l], rpb[l], g_fout[l], g_aout[l], w_o[l], g_ffn[l],
                              w_gate[l], w_up[l], w_down[l])
        return x

    y_prompt = trunk(x_prompt, c_prompt)
    y_sample = trunk(x_sample, c_sample)
    return (y_prompt, y_sample)
```

```python
import functools

import numpy as np
import jax
import jax.numpy as jnp
from jax import lax
from jax.experimental import pallas as pl
from jax.experimental.pallas import tpu as pltpu

F32 = jnp.float32
BF16 = jnp.bfloat16

D_MODEL = 1024
GRID_W = 64
D_FOURIER = 512
N_FGROUPS = 4
FG_DIM = 128
D_NA = 512
N_HEADS = 8
HEAD_DIM = 64
WIN_ROWS = 8
WIN_COLS = 16
D_IN = 2048
D_FF = 2816
N_MOD = 6
EPS = 1e-6

LANES = 128
HEADS_PER_BLOCK = LANES // HEAD_DIM
N_HEAD_BLOCKS = N_HEADS // HEADS_PER_BLOCK
PAIR_ROWS = 2
PAIR_Q = PAIR_ROWS * GRID_W
KEY_ROWS = 10
KEY_N = KEY_ROWS * GRID_W
N_WIN_TYPES = 5
NEG_BIG = -1e30

VMEM_LIMIT = 56 * 1024 * 1024


def _cparams(n_axes):
    return pltpu.CompilerParams(
        dimension_semantics=("arbitrary",) * n_axes, vmem_limit_bytes=VMEM_LIMIT)


def _channel_dft_matrix():
    c = np.arange(FG_DIM)
    ang = 2.0 * np.pi * ((c[:, None] * c[None, :]) % FG_DIM) / FG_DIM
    s = 1.0 / np.sqrt(FG_DIM)
    return np.concatenate([np.cos(ang) * s, np.sin(ang) * s], axis=1)


def _stage_a_matrix(n1):
    k = np.arange(n1)
    ang = 2.0 * np.pi * ((k[:, None] * k[None, :]) % n1) / n1
    c, s = np.cos(ang), np.sin(ang)
    re = np.concatenate([c, -s], axis=1)
    im = np.concatenate([-s, -c], axis=1)
    m = np.stack([re, im], axis=1).reshape(2 * n1, 2 * n1)
    return m / np.sqrt(n1)


def _stage_b_matrices(n1, n2):
    t = n1 * n2
    k1 = np.arange(n1)[:, None, None]
    k2 = np.arange(n2)[None, :, None]
    m = np.arange(n2)[None, None, :]
    idx = (m * k2 * n1 + m * k1) % t
    ang = 2.0 * np.pi * idx / t
    g = np.concatenate([np.cos(ang), np.sin(ang)], axis=2)
    return g / np.sqrt(n2)


def _head_mean_matrix():
    h = np.arange(D_NA) // HEAD_DIM
    return (h[:, None] == h[None, :]).astype(np.float64) / HEAD_DIM


def _window_plan(rows):
    types = {}
    for m in range(rows // PAIR_ROWS):
        i0 = PAIR_ROWS * m
        ws = min(max(i0 - WIN_ROWS // 2, 0), rows - KEY_ROWS)
        e2 = (i0 - ws) // 2
        rel = tuple(min(max(i0 + r - WIN_ROWS // 2, 0), rows - WIN_ROWS) - ws for r in range(PAIR_ROWS))
        assert types.setdefault(e2, rel) == rel
        assert (i0 - ws) % 2 == 0 and all(0 <= x and x + WIN_ROWS <= KEY_ROWS for x in rel)
    assert sorted(types) == list(range(N_WIN_TYPES))
    return types


def _bias_index_tables():
    plan = _window_plan(64)
    assert plan == _window_plan(128)
    kk = np.arange(KEY_N)
    qq = np.arange(PAIR_Q)
    jr, jc = kk // GRID_W, kk % GRID_W
    r, c = qq // GRID_W, qq % GRID_W
    cstart = np.clip(c - WIN_COLS // 2, 0, GRID_W - WIN_COLS)
    col_ok = (jc[:, None] >= cstart[None, :]) & (jc[:, None] < cstart[None, :] + WIN_COLS)
    dc = jc[:, None] - c[None, :] + (WIN_COLS - 1)
    dr_all, ok_all = [], []
    for e2 in range(N_WIN_TYPES):
        rel = np.asarray(plan[e2])[r]
        row_ok = (jr[:, None] >= rel[None, :]) & (jr[:, None] < rel[None, :] + WIN_ROWS)
        dr = jr[:, None] - (2 * e2 + r[None, :]) + (WIN_ROWS - 1)
        ok = row_ok & col_ok
        assert np.all((dr[ok] >= 0) & (dr[ok] <= 2 * WIN_ROWS - 2))
        assert np.all((dc[ok] >= 0) & (dc[ok] <= 2 * WIN_COLS - 2))
        assert np.all(ok.sum(axis=0) == WIN_ROWS * WIN_COLS)
        dr_all.append(np.clip(dr, 0, 2 * WIN_ROWS - 2))
        ok_all.append(ok)
    dc = np.clip(dc, 0, 2 * WIN_COLS - 2)
    return np.stack(dr_all), np.broadcast_to(dc, (N_WIN_TYPES,) + dc.shape), np.stack(ok_all)


_BIAS_DR, _BIAS_DC, _BIAS_OK = _bias_index_tables()


def _adaln_kernel(c_ref, w_ref, b_ref, o_ref):
    c = c_ref[...]
    s = c * jax.nn.sigmoid(c)
    o_ref[...] = jnp.dot(s, w_ref[...], preferred_element_type=F32,
                         precision=lax.Precision.HIGHEST) + b_ref[...]


def _rms(x):
    return x * lax.rsqrt(jnp.mean(x * x, axis=-1, keepdims=True) + EPS)


def _head_rms(t, p):
    sq = t * t
    hi = sq.astype(BF16)
    lo = (sq - hi.astype(F32)).astype(BF16)
    ms = (jnp.dot(hi, p, preferred_element_type=F32) + jnp.dot(lo, p, preferred_element_type=F32))
    return t * lax.rsqrt(ms + EPS)


def _inproj_kernel(x_ref, mod_ref, gattn_ref, win_ref, cs_ref, pm_ref, gq_ref, gk_ref,
                   ua_ref, ub_ref, q_ref, k_ref, v_ref):
    x = x_ref[0]
    m = mod_ref[0]
    h = _rms(x) * gattn_ref[...] * (1.0 + m[1:2]) + m[0:1]
    z = jnp.dot(h.astype(BF16), win_ref[...], preferred_element_type=F32)
    cs = cs_ref[...]
    for g in range(N_FGROUPS):
        ug = z[:, g * FG_DIM:(g + 1) * FG_DIM].astype(BF16)
        ab = jnp.dot(ug, cs, preferred_element_type=F32)
        ua_ref[0, :, g * FG_DIM:(g + 1) * FG_DIM] = ab[:, :FG_DIM].astype(BF16)
        ub_ref[0, :, g * FG_DIM:(g + 1) * FG_DIM] = ab[:, FG_DIM:].astype(BF16)
    p = pm_ref[...]
    q = z[:, D_FOURIER:D_FOURIER + D_NA]
    k = z[:, D_FOURIER + D_NA:D_FOURIER + 2 * D_NA]
    q_ref[0] = (_head_rms(q, p) * gq_ref[...] * (HEAD_DIM ** -0.5)).astype(BF16)
    k_ref[0] = (_head_rms(k, p) * gk_ref[...]).astype(BF16)
    v_ref[0] = z[:, D_FOURIER + 2 * D_NA:].astype(BF16)


def _dft_a_kernel(a_ref, b_ref, f_ref, y_ref):
    x = jnp.concatenate([a_ref[0], b_ref[0]], axis=0)
    y_ref[0] = jnp.dot(f_ref[...], x, preferred_element_type=F32).astype(BF16)


def _dft_b_kernel(y_ref, g_ref, o_ref, *, k1_block, n2):
    for j in range(k1_block):
        y = y_ref[0, j].reshape(2 * n2, D_FOURIER)
        r = jnp.dot(g_ref[j], y, preferred_element_type=F32)
        o_ref[0, :, j * D_FOURIER:(j + 1) * D_FOURIER] = r.astype(BF16)


def _natten_kernel(q_ref, k_ref, v_ref, bias_ref, o_ref, *, rows, pairs):
    mblk = pl.program_id(2)
    lane = lax.broadcasted_iota(jnp.int32, (PAIR_Q, LANES), 1)

    def body(p, carry):
        i0 = PAIR_ROWS * (mblk * pairs + p)
        ws = jnp.clip(i0 - WIN_ROWS // 2, 0, rows - KEY_ROWS)
        wtype = lax.shift_right_logical(i0 - ws, 1)
        kstart = pl.multiple_of(ws * GRID_W, LANES)
        qstart = pl.multiple_of(p * PAIR_Q, PAIR_Q)
        kw = k_ref[0, pl.ds(kstart, KEY_N), :]
        vw = v_ref[0, pl.ds(kstart, KEY_N), :]
        qp = q_ref[0, pl.ds(qstart, PAIR_Q), :].astype(F32)
        outs = []
        for h in range(HEADS_PER_BLOCK):
            in_head = (lane >= h * HEAD_DIM) & (lane < (h + 1) * HEAD_DIM)
            qm = jnp.where(in_head, qp, 0.0).astype(BF16)
            st = lax.dot_general(kw, qm, (((1,), (1,)), ((), ())), preferred_element_type=F32)
            st = st + bias_ref[0, h, wtype]
            mx = jnp.max(st, axis=0, keepdims=True)
            pe = jnp.exp(st - mx)
            den = jnp.sum(pe, axis=0, keepdims=True)
            pn = (pe * (1.0 / den)).astype(BF16)
            outs.append(lax.dot_general(pn, vw, (((0,), (0,)), ((), ())), preferred_element_type=F32))
        out = jnp.where(lane < HEAD_DIM, outs[0], outs[1])
        o_ref[0, pl.ds(qstart, PAIR_Q), :] = out.astype(BF16)
        return carry

    lax.fori_loop(0, pairs, body, 0)


def _outffn_kernel(x_ref, f_ref, a_ref, mod_ref, gf_ref, ga_ref, wf_ref, wo_ref, gffn_ref,
                   wg_ref, wu_ref, wd_ref, o_ref):
    x = x_ref[0]
    m = mod_ref[0]
    f = f_ref[0]
    fo = jnp.concatenate(
        [jnp.dot(f[:, g * FG_DIM:(g + 1) * FG_DIM], wf_ref[g], preferred_element_type=F32)
         for g in range(N_FGROUPS)], axis=-1)
    fn = _rms(fo) * gf_ref[...]
    an = _rms(a_ref[0].astype(F32)) * ga_ref[...]
    cat = jnp.concatenate([fn, an], axis=-1).astype(BF16)
    mix = jnp.dot(cat, wo_ref[...], preferred_element_type=F32)
    x1 = x + m[2:3] * mix
    h2 = (_rms(x1) * gffn_ref[...] * (1.0 + m[4:5]) + m[3:4]).astype(BF16)
    gate = jnp.dot(h2, wg_ref[...], preferred_element_type=F32)
    up = jnp.dot(h2, wu_ref[...], preferred_element_type=F32)
    act = (gate * jax.nn.sigmoid(gate) * up).astype(BF16)
    ff = jnp.dot(act, wd_ref[...], preferred_element_type=F32)
    o_ref[0] = x1 + m[5:6] * ff


def _bf16_const(a):
    return jnp.asarray(a, F32).astype(BF16)


def _const_spec(shape):
    nd = len(shape)
    return pl.BlockSpec(shape, lambda *_: (0,) * nd, pipeline_mode=pl.Buffered(1))


def _adaln(c_all, w_ada, b_ada):
    n = c_all.shape[0]
    tn = 1024
    return pl.pallas_call(
        _adaln_kernel,
        grid=(N_MOD * D_MODEL // tn,),
        in_specs=[pl.BlockSpec((n, D_MODEL), lambda j: (0, 0)),
                  pl.BlockSpec((D_MODEL, tn), lambda j: (0, j)),
                  pl.BlockSpec((1, tn), lambda j: (0, j))],
        out_specs=pl.BlockSpec((n, tn), lambda j: (0, j)),
        out_shape=jax.ShapeDtypeStruct((n, N_MOD * D_MODEL), F32),
        compiler_params=_cparams(1),
        name="adaln",
    )(c_all, w_ada, b_ada.reshape(1, -1))


def _inproj(x, mod, g_attn, w_in, cs, pm, gq, gk, tm):
    b, t, _ = x.shape
    tok = lambda d: pl.BlockSpec((1, tm, d), lambda i, j: (i, j, 0))
    out = jax.ShapeDtypeStruct((b, t, D_FOURIER), BF16)
    return pl.pallas_call(
        _inproj_kernel,
        grid=(b, t // tm),
        in_specs=[tok(D_MODEL),
                  pl.BlockSpec((1, 8, D_MODEL), lambda i, j: (i, 0, 0)),
                  _const_spec((1, D_MODEL)), _const_spec((D_MODEL, D_IN)),
                  _const_spec((FG_DIM, 2 * FG_DIM)), _const_spec((D_NA, D_NA)),
                  _const_spec((1, D_NA)), _const_spec((1, D_NA))],
        out_specs=[tok(D_FOURIER)] * 5,
        out_shape=[out] * 5,
        compiler_params=_cparams(2),
        name="inproj",
    )(x, mod, g_attn, w_in, cs, pm, gq, gk)


def _fourier(ua, ub, n1, n2):
    b, t, d = ua.shape
    assert n1 * n2 == t
    fa = _bf16_const(_stage_a_matrix(n1))
    gm = _bf16_const(_stage_b_matrices(n1, n2))
    lanes = n2 * d
    lchunk = 4096
    y = pl.pallas_call(
        _dft_a_kernel,
        grid=(b, lanes // lchunk),
        in_specs=[pl.BlockSpec((1, n1, lchunk), lambda i, j: (i, 0, j)),
                  pl.BlockSpec((1, n1, lchunk), lambda i, j: (i, 0, j)),
                  _const_spec((2 * n1, 2 * n1))],
        out_specs=pl.BlockSpec((1, 2 * n1, lchunk), lambda i, j: (i, 0, j)),
        out_shape=jax.ShapeDtypeStruct((b, 2 * n1, lanes), BF16),
        compiler_params=_cparams(2),
        name="dft_a",
    )(ua.reshape(b, n1, lanes), ub.reshape(b, n1, lanes), fa)
    k1_block = 8
    out = pl.pallas_call(
        functools.partial(_dft_b_kernel, k1_block=k1_block, n2=n2),
        grid=(b, n1 // k1_block),
        in_specs=[pl.BlockSpec((1, k1_block, 2, n2, d), lambda i, j: (i, j, 0, 0, 0)),
                  pl.BlockSpec((k1_block, n2, 2 * n2), lambda i, j: (j, 0, 0))],
        out_specs=pl.BlockSpec((1, n2, k1_block * d), lambda i, j: (i, 0, j)),
        out_shape=jax.ShapeDtypeStruct((b, n2, n1 * d), BF16),
        compiler_params=_cparams(2),
        name="dft_b",
    )(y.reshape(b, n1, 2, n2, d), gm)
    return out.reshape(b, t, d)


def _natten(q, k, v, bias):
    b, t, _ = q.shape
    rows = t // GRID_W
    pairs = 4
    tq = pairs * PAIR_Q
    return pl.pallas_call(
        functools.partial(_natten_kernel, rows=rows, pairs=pairs),
        grid=(N_HEAD_BLOCKS, b, t // tq),
        in_specs=[pl.BlockSpec((1, tq, LANES), lambda hb, i, j: (i, j, hb)),
                  pl.BlockSpec((1, t, LANES), lambda hb, i, j: (i, 0, hb)),
                  pl.BlockSpec((1, t, LANES), lambda hb, i, j: (i, 0, hb)),
                  pl.BlockSpec((1, HEADS_PER_BLOCK, N_WIN_TYPES, KEY_N, PAIR_Q),
                               lambda hb, i, j: (hb, 0, 0, 0, 0))],
        out_specs=pl.BlockSpec((1, tq, LANES), lambda hb, i, j: (i, j, hb)),
        out_shape=jax.ShapeDtypeStruct((b, t, D_NA), BF16),
        compiler_params=_cparams(3),
        name="natten",
    )(q, k, v, bias)


def _outffn(x, f, a, mod, gf, ga, wf, wo, gffn, wg, wu, wd, tm):
    b, t, _ = x.shape
    tok = lambda d: pl.BlockSpec((1, tm, d), lambda i, j: (i, j, 0))
    return pl.pallas_call(
        _outffn_kernel,
        grid=(b, t // tm),
        in_specs=[tok(D_MODEL), tok(D_FOURIER), tok(D_NA),
                  pl.BlockSpec((1, 8, D_MODEL), lambda i, j: (i, 0, 0)),
                  _const_spec((1, D_FOURIER)), _const_spec((1, D_NA)),
                  _const_spec((N_FGROUPS, FG_DIM, FG_DIM)), _const_spec((D_MODEL, D_MODEL)),
                  _const_spec((1, D_MODEL)), _const_spec((D_MODEL, D_FF)),
                  _const_spec((D_MODEL, D_FF)), _const_spec((D_FF, D_MODEL))],
        out_specs=tok(D_MODEL),
        out_shape=jax.ShapeDtypeStruct((b, t, D_MODEL), F32),
        compiler_params=_cparams(2),
        name="outffn",
    )(x, f, a, mod, gf, ga, wf, wo, gffn, wg, wu, wd)


def kernel(x_prompt, x_sample, c_prompt, c_sample, w_ada, b_ada, g_attn, w_in, g_q, g_k, w_fmix, rpb,
           g_fout, g_aout, w_o, g_ffn, w_gate, w_up, w_down):
    assert w_ada.shape[0] == 1
    nb_p, nb_s = c_prompt.shape[0], c_sample.shape[0]
    n_c = nb_p + nb_s
    c_all = jnp.concatenate([c_prompt, c_sample, jnp.zeros((-n_c % 8, D_MODEL), F32)], axis=0)
    mod = _adaln(c_all, w_ada[0], b_ada[0])[:n_c].reshape(n_c, N_MOD, D_MODEL)
    mod = jnp.pad(mod, ((0, 0), (0, 8 - N_MOD), (0, 0)))

    w_in_b = w_in[0].astype(BF16)
    w_o_b = w_o[0].astype(BF16)
    w_g_b = w_gate[0].astype(BF16)
    w_u_b = w_up[0].astype(BF16)
    w_d_b = w_down[0].astype(BF16)
    w_f_b = w_fmix[0].astype(BF16)
    cs = _bf16_const(_channel_dft_matrix())
    pm = _bf16_const(_head_mean_matrix())
    gq = jnp.tile(g_q[0], N_HEADS).reshape(1, D_NA)
    gk = jnp.tile(g_k[0], N_HEADS).reshape(1, D_NA)
    row = lambda v: v.reshape(1, -1)

    bias = jnp.where(_BIAS_OK[None], rpb[0][:, _BIAS_DR, _BIAS_DC], NEG_BIG).astype(F32)
    bias = bias.reshape(N_HEAD_BLOCKS, HEADS_PER_BLOCK, N_WIN_TYPES, KEY_N, PAIR_Q)

    def trunk(x, mod_g):
        b, t, _ = x.shape
        ua, ub, q, k, v = _inproj(x, mod_g, row(g_attn[0]), w_in_b, cs, pm, gq, gk, tm=512)
        f = _fourier(ua, ub, t // GRID_W, GRID_W)
        a = _natten(q, k, v, bias)
        return _outffn(x, f, a, mod_g, row(g_fout[0]), row(g_aout[0]), w_f_b, w_o_b, row(g_ffn[0]),
                       w_g_b, w_u_b, w_d_b, tm=256)

    return trunk(x_prompt, mod[:nb_p]), trunk(x_sample, mod[nb_p:])
```

```python
import functools

import numpy as np
import jax
import jax.numpy as jnp
from jax import lax
from jax.experimental import pallas as pl
from jax.experimental.pallas import tpu as pltpu

F32 = jnp.float32
BF16 = jnp.bfloat16

D_MODEL = 1024
GRID_W = 64
D_FOURIER = 512
N_FGROUPS = 4
FG_DIM = 128
D_NA = 512
N_HEADS = 8
HEAD_DIM = 64
WIN_ROWS = 8
WIN_COLS = 16
RPB_ROWS = 2 * WIN_ROWS - 1
RPB_COLS = 2 * WIN_COLS - 1
D_IN = 2048
D_FF = 2816
N_MOD = 6
EPS = 1e-6

LANES = 128
BF16_SUBLANES = 16
HEADS_PER_BLOCK = LANES // HEAD_DIM
N_HEAD_BLOCKS = N_HEADS // HEADS_PER_BLOCK
PAIR_ROWS = 2
PAIR_Q = PAIR_ROWS * GRID_W
KEY_ROWS = 10
KEY_N = KEY_ROWS * GRID_W
N_WIN_TYPES = 5
BIAS_ROWS = RPB_ROWS + 1
NEG_BIG = -1e30

VMEM_LIMIT = 56 * 1024 * 1024


def _cparams(n_axes):
    return pltpu.CompilerParams(
        dimension_semantics=("arbitrary",) * n_axes, vmem_limit_bytes=VMEM_LIMIT)


def _channel_dft_matrix():
    c = np.arange(FG_DIM)
    ang = 2.0 * np.pi * ((c[:, None] * c[None, :]) % FG_DIM) / FG_DIM
    s = 1.0 / np.sqrt(FG_DIM)
    return np.concatenate([np.cos(ang) * s, np.sin(ang) * s], axis=1)


def _stage_a_matrix(n1):
    k = np.arange(n1)
    ang = 2.0 * np.pi * ((k[:, None] * k[None, :]) % n1) / n1
    c, s = np.cos(ang), np.sin(ang)
    re = np.concatenate([c, -s], axis=1)
    im = np.concatenate([-s, -c], axis=1)
    m = np.stack([re, im], axis=1).reshape(2 * n1, 2 * n1)
    return m / np.sqrt(n1)


def _stage_b_matrices(n1, n2):
    t = n1 * n2
    k1 = np.arange(n1)[:, None, None]
    k2 = np.arange(n2)[None, :, None]
    m = np.arange(n2)[None, None, :]
    idx = (m * k2 * n1 + m * k1) % t
    ang = 2.0 * np.pi * idx / t
    g = np.concatenate([np.cos(ang), np.sin(ang)], axis=2)
    return g / np.sqrt(n2)


def _head_mean_matrix():
    h = np.arange(D_NA) // HEAD_DIM
    return (h[:, None] == h[None, :]).astype(np.float64) / HEAD_DIM


def _window_plan(rows):
    types = {}
    for m in range(rows // PAIR_ROWS):
        i0 = PAIR_ROWS * m
        ws = min(max(i0 - WIN_ROWS // 2, 0), rows - KEY_ROWS)
        e2 = (i0 - ws) // 2
        rel = tuple(min(max(i0 + r - WIN_ROWS // 2, 0), rows - WIN_ROWS) - ws for r in range(PAIR_ROWS))
        assert types.setdefault(e2, rel) == rel
        assert (i0 - ws) % 2 == 0 and all(0 <= x and x + WIN_ROWS <= KEY_ROWS for x in rel)
    assert sorted(types) == list(range(N_WIN_TYPES))
    return types


_WINDOW_PLAN = _window_plan(64)
assert _WINDOW_PLAN == _window_plan(128)


def _split_bf16(x):
    hi = x.astype(BF16)
    return hi, (x - hi.astype(F32)).astype(BF16)


def _adaln_kernel(c_ref, w_ref, b_ref, o_ref):
    c = c_ref[...]
    s_hi, s_lo = _split_bf16(c * jax.nn.sigmoid(c))
    w_hi, w_lo = _split_bf16(w_ref[...])
    dot = functools.partial(jnp.dot, preferred_element_type=F32)
    o_ref[...] = dot(s_hi, w_hi) + (dot(s_hi, w_lo) + dot(s_lo, w_hi)) + b_ref[...]


def _rms(x):
    return x * lax.rsqrt(jnp.mean(x * x, axis=-1, keepdims=True) + EPS)


def _head_rms(t, p):
    ms = jnp.dot((t * t).astype(BF16), p, preferred_element_type=F32)
    return t * lax.rsqrt(ms + EPS)


def _inproj_kernel(x_ref, mod_ref, gattn_ref, win_ref, cs_ref, pm_ref, gq_ref, gk_ref,
                   ua_ref, ub_ref, q_ref, k_ref, v_ref):
    x = x_ref[0]
    m = mod_ref[0]
    h = (_rms(x) * gattn_ref[...] * (1.0 + m[1:2]) + m[0:1]).astype(BF16)
    z = jnp.dot(h, win_ref[...], preferred_element_type=F32)
    cs = cs_ref[...]
    for g in range(N_FGROUPS):
        ug = z[:, g * FG_DIM:(g + 1) * FG_DIM].astype(BF16)
        ab = jnp.dot(ug, cs, preferred_element_type=F32)
        ua_ref[0, :, g * FG_DIM:(g + 1) * FG_DIM] = ab[:, :FG_DIM].astype(BF16)
        ub_ref[0, :, g * FG_DIM:(g + 1) * FG_DIM] = ab[:, FG_DIM:].astype(BF16)
    p = pm_ref[...]
    q = z[:, D_FOURIER:D_FOURIER + D_NA]
    k = z[:, D_FOURIER + D_NA:D_FOURIER + 2 * D_NA]
    q_ref[0] = (_head_rms(q, p) * gq_ref[...] * (HEAD_DIM ** -0.5)).astype(BF16)
    k_ref[0] = (_head_rms(k, p) * gk_ref[...]).astype(BF16)
    v_ref[0] = z[:, D_FOURIER + 2 * D_NA:].astype(BF16)


def _swap_leading(x):
    return pltpu.einshape("abc->bac", x)


def _dft_a_kernel(a_ref, b_ref, f_ref, y_ref, *, f_block):
    at = _swap_leading(a_ref[0])
    bt = _swap_leading(b_ref[0])
    fa = f_ref[...]
    ys = []
    for f in range(f_block):
        x = jnp.concatenate([at[f], bt[f]], axis=0)
        ys.append(jnp.dot(fa, x, preferred_element_type=F32).astype(BF16))
    y_ref[0] = _swap_leading(jnp.stack(ys, axis=0))


def _dft_b_kernel(y_ref, g_ref, o_ref, *, k1_block, n2):
    rs = []
    for j in range(k1_block):
        y = y_ref[0, j].reshape(2 * n2, D_FOURIER)
        rs.append(jnp.dot(g_ref[j], y, preferred_element_type=F32).astype(BF16))
    o_ref[0] = _swap_leading(jnp.stack(rs, axis=0))


def _build_bias_table(w_ref, tab_ref):
    jc = lax.broadcasted_iota(jnp.int32, (GRID_W, LANES), 0)
    lane = lax.broadcasted_iota(jnp.int32, (GRID_W, LANES), 1)
    c = lane & (GRID_W - 1)
    cstart = jnp.clip(c - WIN_COLS // 2, 0, GRID_W - WIN_COLS)
    col_ok = (jc >= cstart) & (jc < cstart + WIN_COLS)
    first = lane < GRID_W
    neg = jnp.full((GRID_W, LANES), NEG_BIG, F32)
    for h in range(HEADS_PER_BLOCK):
        for wtype in range(N_WIN_TYPES):
            rel = _WINDOW_PLAN[wtype]
            for jr in range(KEY_ROWS):
                ok = [rel[r] <= jr < rel[r] + WIN_ROWS for r in range(PAIR_ROWS)]
                d = jr - PAIR_ROWS * wtype + (WIN_ROWS - 1)
                rows = pl.ds(jr * GRID_W, GRID_W)
                if not any(ok):
                    tab_ref[h, wtype, rows, :] = neg
                    continue
                assert 0 <= d < BIAS_ROWS and (not ok[0] or d < RPB_ROWS) and (not ok[1] or d >= 1)
                x = jnp.broadcast_to(w_ref[0, h, d:d + 1, :], (GRID_W, LANES))
                x = pltpu.roll(x, 0, 1, stride=1, stride_axis=0)
                if all(ok):
                    valid = col_ok
                else:
                    valid = col_ok & (first if ok[0] else jnp.logical_not(first))
                tab_ref[h, wtype, rows, :] = jnp.where(valid, x, neg)


def _natten_kernel(q_ref, k_ref, v_ref, w_ref, o_ref, tab_ref, *, rows, pairs):
    @pl.when((pl.program_id(1) == 0) & (pl.program_id(2) == 0))
    def _():
        _build_bias_table(w_ref, tab_ref)

    mblk = pl.program_id(2)
    lane = lax.broadcasted_iota(jnp.int32, (PAIR_Q, LANES), 1)

    for p in range(pairs):
        i0 = PAIR_ROWS * (mblk * pairs + p)
        ws = jnp.clip(i0 - WIN_ROWS // 2, 0, rows - KEY_ROWS)
        wtype = lax.shift_right_logical(i0 - ws, 1)
        kstart = pl.multiple_of(ws * GRID_W, LANES)
        kw = k_ref[0, pl.ds(kstart, KEY_N), :]
        vw = v_ref[0, pl.ds(kstart, KEY_N), :]
        qp = q_ref[0, p * PAIR_Q:(p + 1) * PAIR_Q, :].astype(F32)
        outs = []
        for h in range(HEADS_PER_BLOCK):
            qm = jnp.where((lane >= h * HEAD_DIM) & (lane < (h + 1) * HEAD_DIM), qp, 0.0).astype(BF16)
            s = lax.dot_general(kw, qm, (((1,), (1,)), ((), ())), preferred_element_type=F32)
            s = s + tab_ref[h, wtype]
            mx = jnp.max(s, axis=0, keepdims=True)
            pe = jnp.exp(s - mx)
            den = jnp.sum(pe, axis=0, keepdims=True)
            pn = (pe * (1.0 / den)).astype(BF16)
            outs.append(lax.dot_general(pn, vw, (((0,), (0,)), ((), ())), preferred_element_type=F32))
        out = jnp.where(lane < HEAD_DIM, outs[0], outs[1])
        o_ref[0, p * PAIR_Q:(p + 1) * PAIR_Q, :] = out.astype(BF16)


def _outffn_kernel(x_ref, f_ref, a_ref, mod_ref, gf_ref, ga_ref, wf_ref, wo_ref, gffn_ref,
                   wg_ref, wu_ref, wd_ref, o_ref):
    x = x_ref[0]
    m = mod_ref[0]
    f = f_ref[0]
    fo = jnp.concatenate(
        [jnp.dot(f[:, g * FG_DIM:(g + 1) * FG_DIM], wf_ref[g], preferred_element_type=F32)
         for g in range(N_FGROUPS)], axis=-1)
    fn = _rms(fo) * gf_ref[...]
    an = _rms(a_ref[0].astype(F32)) * ga_ref[...]
    cat = jnp.concatenate([fn, an], axis=-1).astype(BF16)
    mix = jnp.dot(cat, wo_ref[...], preferred_element_type=F32)
    x1 = x + m[2:3] * mix
    h2 = (_rms(x1) * gffn_ref[...] * (1.0 + m[4:5]) + m[3:4]).astype(BF16)
    gate = jnp.dot(h2, wg_ref[...], preferred_element_type=F32)
    up = jnp.dot(h2, wu_ref[...], preferred_element_type=F32)
    act = (gate * jax.nn.sigmoid(gate) * up).astype(BF16)
    ff = jnp.dot(act, wd_ref[...], preferred_element_type=F32)
    o_ref[0] = x1 + m[5:6] * ff


def _bf16_const(a):
    return jnp.asarray(a, F32).astype(BF16)


def _const_spec(shape):
    nd = len(shape)
    return pl.BlockSpec(shape, lambda *_: (0,) * nd, pipeline_mode=pl.Buffered(1))


def _adaln(c_all, w_ada, b_ada):
    n = c_all.shape[0]
    tn = 1024
    return pl.pallas_call(
        _adaln_kernel,
        grid=(N_MOD * D_MODEL // tn,),
        in_specs=[pl.BlockSpec((n, D_MODEL), lambda j: (0, 0)),
                  pl.BlockSpec((D_MODEL, tn), lambda j: (0, j)),
                  pl.BlockSpec((1, tn), lambda j: (0, j))],
        out_specs=pl.BlockSpec((n, tn), lambda j: (0, j)),
        out_shape=jax.ShapeDtypeStruct((n, N_MOD * D_MODEL), F32),
        compiler_params=_cparams(1),
        name="adaln",
    )(c_all, w_ada, b_ada.reshape(1, -1))


def _inproj(x, mod, g_attn, w_in, cs, pm, gq, gk, tm):
    b, t, _ = x.shape
    tok = lambda d: pl.BlockSpec((1, tm, d), lambda i, j: (i, j, 0))
    out = jax.ShapeDtypeStruct((b, t, D_FOURIER), BF16)
    return pl.pallas_call(
        _inproj_kernel,
        grid=(b, t // tm),
        in_specs=[tok(D_MODEL),
                  pl.BlockSpec((1, 8, D_MODEL), lambda i, j: (i, 0, 0)),
                  _const_spec((1, D_MODEL)), _const_spec((D_MODEL, D_IN)),
                  _const_spec((FG_DIM, 2 * FG_DIM)), _const_spec((D_NA, D_NA)),
                  _const_spec((1, D_NA)), _const_spec((1, D_NA))],
        out_specs=[tok(D_FOURIER)] * 5,
        out_shape=[out] * 5,
        compiler_params=_cparams(2),
        name="inproj",
    )(x, mod, g_attn, w_in, cs, pm, gq, gk)


def _fourier(ua, ub, n1, n2):
    b, t, d = ua.shape
    assert n1 * n2 == t and n2 % BF16_SUBLANES == 0 and n1 % BF16_SUBLANES == 0
    fa = _bf16_const(_stage_a_matrix(n1))
    gm = _bf16_const(_stage_b_matrices(n1, n2))
    f_block = BF16_SUBLANES
    in_spec = pl.BlockSpec((1, n1, f_block, d), lambda i, j: (i, 0, j, 0))
    y = pl.pallas_call(
        functools.partial(_dft_a_kernel, f_block=f_block),
        grid=(b, n2 // f_block),
        in_specs=[in_spec, in_spec, _const_spec((2 * n1, 2 * n1))],
        out_specs=pl.BlockSpec((1, 2 * n1, f_block, d), lambda i, j: (i, 0, j, 0)),
        out_shape=jax.ShapeDtypeStruct((b, 2 * n1, n2, d), BF16),
        compiler_params=_cparams(2),
        name="dft_a",
    )(ua.reshape(b, n1, n2, d), ub.reshape(b, n1, n2, d), fa)
    k1_block = BF16_SUBLANES
    out = pl.pallas_call(
        functools.partial(_dft_b_kernel, k1_block=k1_block, n2=n2),
        grid=(b, n1 // k1_block),
        in_specs=[pl.BlockSpec((1, k1_block, 2, n2, d), lambda i, j: (i, j, 0, 0, 0)),
                  pl.BlockSpec((k1_block, n2, 2 * n2), lambda i, j: (j, 0, 0))],
        out_specs=pl.BlockSpec((1, n2, k1_block, d), lambda i, j: (i, 0, j, 0)),
        out_shape=jax.ShapeDtypeStruct((b, n2, n1, d), BF16),
        compiler_params=_cparams(2),
        name="dft_b",
    )(y.reshape(b, n1, 2, n2, d), gm)
    return out.reshape(b, t, d)


def _natten(q, k, v, bias_rows):
    b, t, _ = q.shape
    rows = t // GRID_W
    pairs = 4
    tq = pairs * PAIR_Q
    return pl.pallas_call(
        functools.partial(_natten_kernel, rows=rows, pairs=pairs),
        grid=(N_HEAD_BLOCKS, b, t // tq),
        in_specs=[pl.BlockSpec((1, tq, LANES), lambda hb, i, j: (i, j, hb)),
                  pl.BlockSpec((1, t, LANES), lambda hb, i, j: (i, 0, hb)),
                  pl.BlockSpec((1, t, LANES), lambda hb, i, j: (i, 0, hb)),
                  pl.BlockSpec((1, HEADS_PER_BLOCK, BIAS_ROWS, LANES), lambda hb, i, j: (hb, 0, 0, 0))],
        out_specs=pl.BlockSpec((1, tq, LANES), lambda hb, i, j: (i, j, hb)),
        out_shape=jax.ShapeDtypeStruct((b, t, D_NA), BF16),
        scratch_shapes=[pltpu.VMEM((HEADS_PER_BLOCK, N_WIN_TYPES, KEY_N, PAIR_Q), F32)],
        compiler_params=_cparams(3),
        name="natten",
    )(q, k, v, bias_rows)


def _outffn(x, f, a, mod, gf, ga, wf, wo, gffn, wg, wu, wd, tm):
    b, t, _ = x.shape
    tok = lambda d: pl.BlockSpec((1, tm, d), lambda i, j: (i, j, 0))
    return pl.pallas_call(
        _outffn_kernel,
        grid=(b, t // tm),
        in_specs=[tok(D_MODEL), tok(D_FOURIER), tok(D_NA),
                  pl.BlockSpec((1, 8, D_MODEL), lambda i, j: (i, 0, 0)),
                  _const_spec((1, D_FOURIER)), _const_spec((1, D_NA)),
                  _const_spec((N_FGROUPS, FG_DIM, FG_DIM)), _const_spec((D_MODEL, D_MODEL)),
                  _const_spec((1, D_MODEL)), _const_spec((D_MODEL, D_FF)),
                  _const_spec((D_MODEL, D_FF)), _const_spec((D_FF, D_MODEL))],
        out_specs=tok(D_MODEL),
        out_shape=jax.ShapeDtypeStruct((b, t, D_MODEL), F32),
        compiler_params=_cparams(2),
        name="outffn",
    )(x, f, a, mod, gf, ga, wf, wo, gffn, wg, wu, wd)


def _stage_bias_rows(rpb):
    rev = rpb[:, :, ::-1]
    r0 = jnp.pad(rev, ((0, 0), (0, 1), (0, 0)))
    r1 = jnp.pad(rev, ((0, 0), (1, 0), (0, 0)))
    half = WIN_COLS - 1
    gap = GRID_W - RPB_COLS
    z = jnp.zeros((N_HEADS, BIAS_ROWS, gap), rpb.dtype)
    w = jnp.concatenate([r0[..., half:], z, r1, z, r0[..., :half]], axis=-1)
    assert w.shape == (N_HEADS, BIAS_ROWS, LANES)
    return w.reshape(N_HEAD_BLOCKS, HEADS_PER_BLOCK, BIAS_ROWS, LANES)


def kernel(x_prompt, x_sample, c_prompt, c_sample, w_ada, b_ada, g_attn, w_in, g_q, g_k, w_fmix, rpb,
           g_fout, g_aout, w_o, g_ffn, w_gate, w_up, w_down):
    assert w_ada.shape[0] == 1
    nb_p, nb_s = c_prompt.shape[0], c_sample.shape[0]
    n_c = nb_p + nb_s
    c_all = jnp.concatenate([c_prompt, c_sample, jnp.zeros((-n_c % 8, D_MODEL), F32)], axis=0)
    mod = _adaln(c_all, w_ada[0], b_ada[0])[:n_c].reshape(n_c, N_MOD, D_MODEL)
    mod = jnp.pad(mod, ((0, 0), (0, 8 - N_MOD), (0, 0)))

    w_in_b = w_in[0].astype(BF16)
    w_o_b = w_o[0].astype(BF16)
    w_g_b = w_gate[0].astype(BF16)
    w_u_b = w_up[0].astype(BF16)
    w_d_b = w_down[0].astype(BF16)
    w_f_b = w_fmix[0].astype(BF16)
    cs = _bf16_const(_channel_dft_matrix())
    pm = _bf16_const(_head_mean_matrix())
    gq = jnp.tile(g_q[0], N_HEADS).reshape(1, D_NA)
    gk = jnp.tile(g_k[0], N_HEADS).reshape(1, D_NA)
    row = lambda v: v.reshape(1, -1)
    bias_rows = _stage_bias_rows(rpb[0])

    def trunk(x, mod_g):
        b, t, _ = x.shape
        ua, ub, q, k, v = _inproj(x, mod_g, row(g_attn[0]), w_in_b, cs, pm, gq, gk, tm=512)
        f = _fourier(ua, ub, t // GRID_W, GRID_W)
        a = _natten(q, k, v, bias_rows)
        return _outffn(x, f, a, mod_g, row(g_fout[0]), row(g_aout[0]), w_f_b, w_o_b, row(g_ffn[0]),
                       w_g_b, w_u_b, w_d_b, tm=256)

    return trunk(x_prompt, mod[:nb_p]), trunk(x_sample, mod[nb_p:])
```

```python
import functools

import numpy as np
import jax
import jax.numpy as jnp
from jax import lax
from jax.experimental import pallas as pl
from jax.experimental.pallas import tpu as pltpu

F32 = jnp.float32
BF16 = jnp.bfloat16

D_MODEL = 1024
GRID_W = 64
D_FOURIER = 512
N_FGROUPS = 4
FG_DIM = 128
D_NA = 512
N_HEADS = 8
HEAD_DIM = 64
WIN_ROWS = 8
WIN_COLS = 16
RPB_ROWS = 2 * WIN_ROWS - 1
RPB_COLS = 2 * WIN_COLS - 1
D_IN = 2048
D_FF = 2816
N_MOD = 6
EPS = 1e-6

LANES = 128
HEAD_MEAN_W = 256
BF16_SUBLANES = 16
HEADS_PER_BLOCK = LANES // HEAD_DIM
N_HEAD_BLOCKS = N_HEADS // HEADS_PER_BLOCK
PAIR_ROWS = 2
PAIR_Q = PAIR_ROWS * GRID_W
KEY_ROWS = 10
KEY_N = KEY_ROWS * GRID_W
N_WIN_TYPES = 5
BIAS_ROWS = RPB_ROWS + 1
NEG_BIG = -1e30
LOG2E = 1.4426950408889634

VMEM_LIMIT = 56 * 1024 * 1024


def _cparams(n_axes):
    return pltpu.CompilerParams(
        dimension_semantics=("arbitrary",) * n_axes, vmem_limit_bytes=VMEM_LIMIT)


def _channel_dft_matrix():
    c = np.arange(FG_DIM)
    ang = 2.0 * np.pi * ((c[:, None] * c[None, :]) % FG_DIM) / FG_DIM
    s = 1.0 / np.sqrt(FG_DIM)
    return np.concatenate([np.cos(ang) * s, np.sin(ang) * s], axis=1)


def _stage_a_matrix(n1):
    k = np.arange(n1)
    ang = 2.0 * np.pi * ((k[:, None] * k[None, :]) % n1) / n1
    c, s = np.cos(ang), np.sin(ang)
    re = np.concatenate([c, -s], axis=1)
    im = np.concatenate([-s, -c], axis=1)
    m = np.stack([re, im], axis=1).reshape(2 * n1, 2 * n1)
    return m / np.sqrt(n1)


def _stage_b_matrices(n1, n2):
    t = n1 * n2
    k1 = np.arange(n1)[:, None, None]
    k2 = np.arange(n2)[None, :, None]
    m = np.arange(n2)[None, None, :]
    idx = (m * k2 * n1 + m * k1) % t
    ang = 2.0 * np.pi * idx / t
    g = np.concatenate([np.cos(ang), np.sin(ang)], axis=2)
    return g / np.sqrt(n2)


def _head_mean_matrix():
    h = np.arange(HEAD_MEAN_W) // HEAD_DIM
    return (h[:, None] == h[None, :]).astype(np.float64) / HEAD_DIM


def _window_plan(rows):
    types = {}
    for m in range(rows // PAIR_ROWS):
        i0 = PAIR_ROWS * m
        ws = min(max(i0 - WIN_ROWS // 2, 0), rows - KEY_ROWS)
        e2 = (i0 - ws) // 2
        rel = tuple(min(max(i0 + r - WIN_ROWS // 2, 0), rows - WIN_ROWS) - ws for r in range(PAIR_ROWS))
        assert types.setdefault(e2, rel) == rel
        assert (i0 - ws) % 2 == 0 and all(0 <= x and x + WIN_ROWS <= KEY_ROWS for x in rel)
    assert sorted(types) == list(range(N_WIN_TYPES))
    return types


_WINDOW_PLAN = _window_plan(64)
assert _WINDOW_PLAN == _window_plan(128)


def _split_bf16(x):
    hi = x.astype(BF16)
    return hi, (x - hi.astype(F32)).astype(BF16)


def _adaln_kernel(c_ref, w_ref, b_ref, o_ref):
    c = c_ref[...]
    s_hi, s_lo = _split_bf16(c * jax.nn.sigmoid(c))
    w_hi, w_lo = _split_bf16(w_ref[...])
    dot = functools.partial(jnp.dot, preferred_element_type=F32)
    o_ref[...] = dot(s_hi, w_hi) + (dot(s_hi, w_lo) + dot(s_lo, w_hi)) + b_ref[...]


def _rms(x):
    return x * lax.rsqrt(jnp.mean(x * x, axis=-1, keepdims=True) + EPS)


def _head_rms(t, p):
    sq = (t * t).astype(BF16)
    ms = jnp.concatenate(
        [jnp.dot(sq[:, c:c + HEAD_MEAN_W], p, preferred_element_type=F32) for c in range(0, D_NA, HEAD_MEAN_W)],
        axis=-1)
    return t * lax.rsqrt(ms + EPS)


def _inproj_kernel(x_ref, mod_ref, gattn_ref, win_ref, cs_ref, pm_ref, gq_ref, gk_ref,
                   ua_ref, ub_ref, q_ref, k_ref, v_ref):
    x = x_ref[0]
    m = mod_ref[0]
    h = (_rms(x) * gattn_ref[...] * (1.0 + m[1:2]) + m[0:1]).astype(BF16)
    z = jnp.dot(h, win_ref[...], preferred_element_type=F32)
    cs = cs_ref[...]
    for g in range(N_FGROUPS):
        ug = z[:, g * FG_DIM:(g + 1) * FG_DIM].astype(BF16)
        ab = jnp.dot(ug, cs, preferred_element_type=F32)
        ua_ref[0, :, g * FG_DIM:(g + 1) * FG_DIM] = ab[:, :FG_DIM].astype(BF16)
        ub_ref[0, :, g * FG_DIM:(g + 1) * FG_DIM] = ab[:, FG_DIM:].astype(BF16)
    p = pm_ref[...]
    q = z[:, D_FOURIER:D_FOURIER + D_NA]
    k = z[:, D_FOURIER + D_NA:D_FOURIER + 2 * D_NA]
    q_ref[0] = (_head_rms(q, p) * gq_ref[...] * (HEAD_DIM ** -0.5 * LOG2E)).astype(BF16)
    k_ref[0] = (_head_rms(k, p) * gk_ref[...]).astype(BF16)
    v_ref[0] = z[:, D_FOURIER + 2 * D_NA:].astype(BF16)


def _swap_leading(x):
    return jnp.swapaxes(x, 0, 1)


def _dft_a_kernel(a_ref, b_ref, f_ref, y_ref, *, f_block):
    at = _swap_leading(a_ref[0])
    bt = _swap_leading(b_ref[0])
    fa = f_ref[...]
    ys = []
    for f in range(f_block):
        x = jnp.concatenate([at[f], bt[f]], axis=0)
        ys.append(jnp.dot(fa, x, preferred_element_type=F32).astype(BF16))
    y_ref[0] = _swap_leading(jnp.stack(ys, axis=0))


def _dft_b_kernel(y_ref, g_ref, o_ref, *, k1_block, n2):
    rs = []
    for j in range(k1_block):
        y = y_ref[0, j].reshape(2 * n2, D_FOURIER)
        rs.append(jnp.dot(g_ref[j], y, preferred_element_type=F32).astype(BF16))
    o_ref[0] = _swap_leading(jnp.stack(rs, axis=0))


def _build_bias_table(w_ref, tab_ref):
    jc = lax.broadcasted_iota(jnp.int32, (GRID_W, LANES), 0)
    lane = lax.broadcasted_iota(jnp.int32, (GRID_W, LANES), 1)
    c = lane & (GRID_W - 1)
    cstart = jnp.clip(c - WIN_COLS // 2, 0, GRID_W - WIN_COLS)
    col_ok = (jc >= cstart) & (jc < cstart + WIN_COLS)
    first = lane < GRID_W
    neg = jnp.full((GRID_W, LANES), NEG_BIG, F32)
    for h in range(HEADS_PER_BLOCK):
        for wtype in range(N_WIN_TYPES):
            rel = _WINDOW_PLAN[wtype]
            for jr in range(KEY_ROWS):
                ok = [rel[r] <= jr < rel[r] + WIN_ROWS for r in range(PAIR_ROWS)]
                d = jr - PAIR_ROWS * wtype + (WIN_ROWS - 1)
                rows = pl.ds(jr * GRID_W, GRID_W)
                if not any(ok):
                    tab_ref[h, wtype, rows, :] = neg
                    continue
                assert 0 <= d < BIAS_ROWS and (not ok[0] or d < RPB_ROWS) and (not ok[1] or d >= 1)
                x = jnp.broadcast_to(w_ref[0, h, d:d + 1, :], (GRID_W, LANES))
                x = pltpu.roll(x, 0, 1, stride=1, stride_axis=0)
                if all(ok):
                    valid = col_ok
                else:
                    valid = col_ok & (first if ok[0] else jnp.logical_not(first))
                tab_ref[h, wtype, rows, :] = jnp.where(valid, x * LOG2E, neg)


def _natten_kernel(q_ref, k_ref, v_ref, w_ref, o_ref, tab_ref, va_ref, vb_ref, *, rows, pairs):
    @pl.when((pl.program_id(1) == 0) & (pl.program_id(2) == 0))
    def _():
        _build_bias_table(w_ref, tab_ref)

    @pl.when(pl.program_id(2) == 0)
    def _():
        v = v_ref[0]
        first_v = lax.broadcasted_iota(jnp.int32, v.shape, 1) < HEAD_DIM
        one = jnp.ones_like(v)
        va_ref[...] = jnp.where(first_v, v, one)
        vb_ref[...] = jnp.where(first_v, one, v)

    mblk = pl.program_id(2)
    lane = lax.broadcasted_iota(jnp.int32, (PAIR_Q, LANES), 1)
    first = lane < HEAD_DIM
    vrefs = (va_ref, vb_ref)

    for p in range(pairs):
        i0 = PAIR_ROWS * (mblk * pairs + p)
        ws = jnp.clip(i0 - WIN_ROWS // 2, 0, rows - KEY_ROWS)
        wtype = lax.shift_right_logical(i0 - ws, 1)
        kstart = pl.multiple_of(ws * GRID_W, LANES)
        kw = k_ref[0, pl.ds(kstart, KEY_N), :]
        qp = q_ref[0, p * PAIR_Q:(p + 1) * PAIR_Q, :].astype(F32)
        outs = []
        for h in range(HEADS_PER_BLOCK):
            qm = jnp.where(first if h == 0 else jnp.logical_not(first), qp, 0.0).astype(BF16)
            s = lax.dot_general(kw, qm, (((1,), (1,)), ((), ())), preferred_element_type=F32)
            s = s + tab_ref[h, wtype]
            mx = jnp.max(s, axis=0, keepdims=True)
            pe = jnp.exp2(s - mx).astype(BF16)
            vw = vrefs[h][pl.ds(kstart, KEY_N), :]
            outs.append(lax.dot_general(pe, vw, (((0,), (0,)), ((), ())), preferred_element_type=F32))
        num = jnp.where(first, outs[0], outs[1])
        den = pltpu.roll(jnp.where(first, outs[1], outs[0]), HEAD_DIM, 1)
        o_ref[0, p * PAIR_Q:(p + 1) * PAIR_Q, :] = (num * (1.0 / den)).astype(BF16)


def _outffn_kernel(x_ref, f_ref, a_ref, mod_ref, gf_ref, ga_ref, wf_ref, wo_ref, gffn_ref,
                   wg_ref, wu_ref, wd_ref, o_ref):
    x = x_ref[0]
    m = mod_ref[0]
    f = f_ref[0]
    fo = jnp.concatenate(
        [jnp.dot(f[:, g * FG_DIM:(g + 1) * FG_DIM], wf_ref[g], preferred_element_type=F32)
         for g in range(N_FGROUPS)], axis=-1)
    fn = _rms(fo) * gf_ref[...]
    an = _rms(a_ref[0].astype(F32)) * ga_ref[...]
    cat = jnp.concatenate([fn, an], axis=-1).astype(BF16)
    mix = jnp.dot(cat, wo_ref[...], preferred_element_type=F32)
    x1 = x + m[2:3] * mix
    h2 = (_rms(x1) * gffn_ref[...] * (1.0 + m[4:5]) + m[3:4]).astype(BF16)
    gate = jnp.dot(h2, wg_ref[...], preferred_element_type=F32)
    up = jnp.dot(h2, wu_ref[...], preferred_element_type=F32)
    act = (gate * jax.nn.sigmoid(gate) * up).astype(BF16)
    ff = jnp.dot(act, wd_ref[...], preferred_element_type=F32)
    o_ref[0] = x1 + m[5:6] * ff


def _bf16_const(a):
    return jnp.asarray(a, F32).astype(BF16)


def _const_spec(shape):
    nd = len(shape)
    return pl.BlockSpec(shape, lambda *_: (0,) * nd, pipeline_mode=pl.Buffered(1))


def _adaln(c_all, w_ada, b_ada):
    n = c_all.shape[0]
    tn = 1024
    return pl.pallas_call(
        _adaln_kernel,
        grid=(N_MOD * D_MODEL // tn,),
        in_specs=[pl.BlockSpec((n, D_MODEL), lambda j: (0, 0)),
                  pl.BlockSpec((D_MODEL, tn), lambda j: (0, j)),
                  pl.BlockSpec((1, tn), lambda j: (0, j))],
        out_specs=pl.BlockSpec((n, tn), lambda j: (0, j)),
        out_shape=jax.ShapeDtypeStruct((n, N_MOD * D_MODEL), F32),
        compiler_params=_cparams(1),
        name="adaln",
    )(c_all, w_ada, b_ada.reshape(1, -1))


def _inproj(x, mod, g_attn, w_in, cs, pm, gq, gk, tm):
    b, t, _ = x.shape
    tok = lambda d: pl.BlockSpec((1, tm, d), lambda i, j: (i, j, 0))
    out = jax.ShapeDtypeStruct((b, t, D_FOURIER), BF16)
    return pl.pallas_call(
        _inproj_kernel,
        grid=(b, t // tm),
        in_specs=[tok(D_MODEL),
                  pl.BlockSpec((1, 8, D_MODEL), lambda i, j: (i, 0, 0)),
                  _const_spec((1, D_MODEL)), _const_spec((D_MODEL, D_IN)),
                  _const_spec((FG_DIM, 2 * FG_DIM)), _const_spec((HEAD_MEAN_W, HEAD_MEAN_W)),
                  _const_spec((1, D_NA)), _const_spec((1, D_NA))],
        out_specs=[tok(D_FOURIER)] * 5,
        out_shape=[out] * 5,
        compiler_params=_cparams(2),
        name="inproj",
    )(x, mod, g_attn, w_in, cs, pm, gq, gk)


def _fourier(ua, ub, n1, n2):
    b, t, d = ua.shape
    assert n1 * n2 == t and n2 % BF16_SUBLANES == 0 and n1 % BF16_SUBLANES == 0
    fa = _bf16_const(_stage_a_matrix(n1))
    gm = _bf16_const(_stage_b_matrices(n1, n2))
    f_block = BF16_SUBLANES
    in_spec = pl.BlockSpec((1, n1, f_block, d), lambda i, j: (i, 0, j, 0))
    y = pl.pallas_call(
        functools.partial(_dft_a_kernel, f_block=f_block),
        grid=(b, n2 // f_block),
        in_specs=[in_spec, in_spec, _const_spec((2 * n1, 2 * n1))],
        out_specs=pl.BlockSpec((1, 2 * n1, f_block, d), lambda i, j: (i, 0, j, 0)),
        out_shape=jax.ShapeDtypeStruct((b, 2 * n1, n2, d), BF16),
        compiler_params=_cparams(2),
        name="dft_a",
    )(ua.reshape(b, n1, n2, d), ub.reshape(b, n1, n2, d), fa)
    k1_block = BF16_SUBLANES
    out = pl.pallas_call(
        functools.partial(_dft_b_kernel, k1_block=k1_block, n2=n2),
        grid=(b, n1 // k1_block),
        in_specs=[pl.BlockSpec((1, k1_block, 2, n2, d), lambda i, j: (i, j, 0, 0, 0)),
                  pl.BlockSpec((k1_block, n2, 2 * n2), lambda i, j: (j, 0, 0))],
        out_specs=pl.BlockSpec((1, n2, k1_block, d), lambda i, j: (i, 0, j, 0)),
        out_shape=jax.ShapeDtypeStruct((b, n2, n1, d), BF16),
        compiler_params=_cparams(2),
        name="dft_b",
    )(y.reshape(b, n1, 2, n2, d), gm)
    return out.reshape(b, t, d)


def _natten(q, k, v, bias_rows):
    b, t, _ = q.shape
    rows = t // GRID_W
    pairs = 16
    tq = pairs * PAIR_Q
    return pl.pallas_call(
        functools.partial(_natten_kernel, rows=rows, pairs=pairs),
        grid=(N_HEAD_BLOCKS, b, t // tq),
        in_specs=[pl.BlockSpec((1, tq, LANES), lambda hb, i, j: (i, j, hb)),
                  pl.BlockSpec((1, t, LANES), lambda hb, i, j: (i, 0, hb)),
                  pl.BlockSpec((1, t, LANES), lambda hb, i, j: (i, 0, hb)),
                  pl.BlockSpec((1, HEADS_PER_BLOCK, BIAS_ROWS, LANES), lambda hb, i, j: (hb, 0, 0, 0))],
        out_specs=pl.BlockSpec((1, tq, LANES), lambda hb, i, j: (i, j, hb)),
        out_shape=jax.ShapeDtypeStruct((b, t, D_NA), BF16),
        scratch_shapes=[pltpu.VMEM((HEADS_PER_BLOCK, N_WIN_TYPES, KEY_N, PAIR_Q), F32),
                        pltpu.VMEM((t, LANES), BF16), pltpu.VMEM((t, LANES), BF16)],
        compiler_params=_cparams(3),
        name="natten",
    )(q, k, v, bias_rows)


def _outffn(x, f, a, mod, gf, ga, wf, wo, gffn, wg, wu, wd, tm):
    b, t, _ = x.shape
    tok = lambda d: pl.BlockSpec((1, tm, d), lambda i, j: (i, j, 0))
    return pl.pallas_call(
        _outffn_kernel,
        grid=(b, t // tm),
        in_specs=[tok(D_MODEL), tok(D_FOURIER), tok(D_NA),
                  pl.BlockSpec((1, 8, D_MODEL), lambda i, j: (i, 0, 0)),
                  _const_spec((1, D_FOURIER)), _const_spec((1, D_NA)),
                  _const_spec((N_FGROUPS, FG_DIM, FG_DIM)), _const_spec((D_MODEL, D_MODEL)),
                  _const_spec((1, D_MODEL)), _const_spec((D_MODEL, D_FF)),
                  _const_spec((D_MODEL, D_FF)), _const_spec((D_FF, D_MODEL))],
        out_specs=tok(D_MODEL),
        out_shape=jax.ShapeDtypeStruct((b, t, D_MODEL), F32),
        compiler_params=_cparams(2),
        name="outffn",
    )(x, f, a, mod, gf, ga, wf, wo, gffn, wg, wu, wd)


def _stage_bias_rows(rpb):
    rev = rpb[:, :, ::-1]
    r0 = jnp.pad(rev, ((0, 0), (0, 1), (0, 0)))
    r1 = jnp.pad(rev, ((0, 0), (1, 0), (0, 0)))
    half = WIN_COLS - 1
    gap = GRID_W - RPB_COLS
    z = jnp.zeros((N_HEADS, BIAS_ROWS, gap), rpb.dtype)
    w = jnp.concatenate([r0[..., half:], z, r1, z, r0[..., :half]], axis=-1)
    assert w.shape == (N_HEADS, BIAS_ROWS, LANES)
    return w.reshape(N_HEAD_BLOCKS, HEADS_PER_BLOCK, BIAS_ROWS, LANES)


def kernel(x_prompt, x_sample, c_prompt, c_sample, w_ada, b_ada, g_attn, w_in, g_q, g_k, w_fmix, rpb,
           g_fout, g_aout, w_o, g_ffn, w_gate, w_up, w_down):
    assert w_ada.shape[0] == 1
    nb_p, nb_s = c_prompt.shape[0], c_sample.shape[0]
    n_c = nb_p + nb_s
    c_all = jnp.concatenate([c_prompt, c_sample, jnp.zeros((-n_c % 8, D_MODEL), F32)], axis=0)
    mod = _adaln(c_all, w_ada[0], b_ada[0])[:n_c].reshape(n_c, N_MOD, D_MODEL)
    mod = jnp.pad(mod, ((0, 0), (0, 8 - N_MOD), (0, 0)))

    w_in_b = w_in[0].astype(BF16)
    w_o_b = w_o[0].astype(BF16)
    w_g_b = w_gate[0].astype(BF16)
    w_u_b = w_up[0].astype(BF16)
    w_d_b = w_down[0].astype(BF16)
    w_f_b = w_fmix[0].astype(BF16)
    cs = _bf16_const(_channel_dft_matrix())
    pm = _bf16_const(_head_mean_matrix())
    gq = jnp.tile(g_q[0], N_HEADS).reshape(1, D_NA)
    gk = jnp.tile(g_k[0], N_HEADS).reshape(1, D_NA)
    row = lambda v: v.reshape(1, -1)
    bias_rows = _stage_bias_rows(rpb[0])

    def trunk(x, mod_g):
        b, t, _ = x.shape
        ua, ub, q, k, v = _inproj(x, mod_g, row(g_attn[0]), w_in_b, cs, pm, gq, gk, tm=1024)
        f = _fourier(ua, ub, t // GRID_W, GRID_W)
        a = _natten(q, k, v, bias_rows)
        return _outffn(x, f, a, mod_g, row(g_fout[0]), row(g_aout[0]), w_f_b, w_o_b, row(g_ffn[0]),
                       w_g_b, w_u_b, w_d_b, tm=512)

    return trunk(x_prompt, mod[:nb_p]), trunk(x_sample, mod[nb_p:])
```

```python
import functools

import numpy as np
import jax
import jax.numpy as jnp
from jax import lax
from jax.experimental import pallas as pl
from jax.experimental.pallas import tpu as pltpu

F32 = jnp.float32
BF16 = jnp.bfloat16

D_MODEL = 1024
GRID_W = 64
D_FOURIER = 512
N_FGROUPS = 4
FG_DIM = 128
D_NA = 512
N_HEADS = 8
HEAD_DIM = 64
WIN_ROWS = 8
WIN_COLS = 16
RPB_ROWS = 2 * WIN_ROWS - 1
RPB_COLS = 2 * WIN_COLS - 1
D_IN = 2048
D_FF = 2816
N_MOD = 6
EPS = 1e-6

LANES = 128
HEAD_MEAN_W = 256
BF16_SUBLANES = 16
HEADS_PER_BLOCK = LANES // HEAD_DIM
N_HEAD_BLOCKS = N_HEADS // HEADS_PER_BLOCK
PAIR_ROWS = 2
PAIR_Q = PAIR_ROWS * GRID_W
KEY_ROWS = 10
KEY_N = KEY_ROWS * GRID_W
N_WIN_TYPES = 5
BIAS_ROWS = RPB_ROWS + 1
NEG_BIG = -1e30
LOG2E = 1.4426950408889634
OUTFFN_ROWS = 512

VMEM_LIMIT = 56 * 1024 * 1024


def _cparams(n_axes):
    return pltpu.CompilerParams(
        dimension_semantics=("arbitrary",) * n_axes, vmem_limit_bytes=VMEM_LIMIT)


def _channel_dft_matrix():
    c = np.arange(FG_DIM)
    ang = 2.0 * np.pi * ((c[:, None] * c[None, :]) % FG_DIM) / FG_DIM
    s = 1.0 / np.sqrt(FG_DIM)
    return np.concatenate([np.cos(ang) * s, np.sin(ang) * s], axis=1)


def _stage_a_matrix(n1):
    k = np.arange(n1)
    ang = 2.0 * np.pi * ((k[:, None] * k[None, :]) % n1) / n1
    c, s = np.cos(ang), np.sin(ang)
    re = np.concatenate([c, -s], axis=1)
    im = np.concatenate([-s, -c], axis=1)
    m = np.stack([re, im], axis=1).reshape(2 * n1, 2 * n1)
    return m / np.sqrt(n1)


def _stage_b_matrices(n1, n2):
    t = n1 * n2
    k1 = np.arange(n1)[:, None, None]
    k2 = np.arange(n2)[None, :, None]
    m = np.arange(n2)[None, None, :]
    idx = (m * k2 * n1 + m * k1) % t
    ang = 2.0 * np.pi * idx / t
    g = np.concatenate([np.cos(ang), np.sin(ang)], axis=2)
    return g / np.sqrt(n2)


def _head_mean_matrix():
    h = np.arange(HEAD_MEAN_W) // HEAD_DIM
    return (h[:, None] == h[None, :]).astype(np.float64) / HEAD_DIM


def _window_plan(rows):
    types = {}
    for m in range(rows // PAIR_ROWS):
        i0 = PAIR_ROWS * m
        ws = min(max(i0 - WIN_ROWS // 2, 0), rows - KEY_ROWS)
        e2 = (i0 - ws) // 2
        rel = tuple(min(max(i0 + r - WIN_ROWS // 2, 0), rows - WIN_ROWS) - ws for r in range(PAIR_ROWS))
        assert types.setdefault(e2, rel) == rel
        assert (i0 - ws) % 2 == 0 and all(0 <= x and x + WIN_ROWS <= KEY_ROWS for x in rel)
    assert sorted(types) == list(range(N_WIN_TYPES))
    return types


_WINDOW_PLAN = _window_plan(64)
assert _WINDOW_PLAN == _window_plan(128)


def _split_bf16(x):
    hi = x.astype(BF16)
    return hi, (x - hi.astype(F32)).astype(BF16)


def _adaln_kernel(c_ref, w_ref, b_ref, o_ref):
    c = c_ref[...]
    s_hi, s_lo = _split_bf16(c * jax.nn.sigmoid(c))
    w_hi, w_lo = _split_bf16(w_ref[...])
    dot = functools.partial(jnp.dot, preferred_element_type=F32)
    o_ref[...] = dot(s_hi, w_hi) + (dot(s_hi, w_lo) + dot(s_lo, w_hi)) + b_ref[...]


def _rms(x):
    return x * lax.rsqrt(jnp.mean(x * x, axis=-1, keepdims=True) + EPS)


def _head_rms(t, p):
    sq = (t * t).astype(BF16)
    ms = jnp.concatenate(
        [jnp.dot(sq[:, c:c + HEAD_MEAN_W], p, preferred_element_type=F32) for c in range(0, D_NA, HEAD_MEAN_W)],
        axis=-1)
    return t * lax.rsqrt(ms + EPS)


def _inproj_kernel(x_ref, mod_ref, gattn_ref, win_ref, cs_ref, pm_ref, gq_ref, gk_ref,
                   ua_ref, ub_ref, q_ref, k_ref, v_ref):
    x = x_ref[0]
    m = mod_ref[0]
    h = (_rms(x) * gattn_ref[...] * (1.0 + m[1:2]) + m[0:1]).astype(BF16)
    z = jnp.dot(h, win_ref[...], preferred_element_type=F32)
    cs = cs_ref[...]
    for g in range(N_FGROUPS):
        ug = z[:, g * FG_DIM:(g + 1) * FG_DIM].astype(BF16)
        ab = jnp.dot(ug, cs, preferred_element_type=F32)
        ua_ref[0, :, g * FG_DIM:(g + 1) * FG_DIM] = ab[:, :FG_DIM].astype(BF16)
        ub_ref[0, :, g * FG_DIM:(g + 1) * FG_DIM] = ab[:, FG_DIM:].astype(BF16)
    p = pm_ref[...]
    q = z[:, D_FOURIER:D_FOURIER + D_NA]
    k = z[:, D_FOURIER + D_NA:D_FOURIER + 2 * D_NA]
    q_ref[0] = (_head_rms(q, p) * gq_ref[...] * (HEAD_DIM ** -0.5 * LOG2E)).astype(BF16)
    k_ref[0] = (_head_rms(k, p) * gk_ref[...]).astype(BF16)
    v_ref[0] = z[:, D_FOURIER + 2 * D_NA:].astype(BF16)


def _swap_leading(x):
    return jnp.swapaxes(x, 0, 1)


def _dft_kernel(a_ref, b_ref, f_ref, g_ref, o_ref, y_ref, *, n_a_steps, f_block, k1_block, n2):
    j = pl.program_id(1)

    @pl.when(j < n_a_steps)
    def _():
        at = _swap_leading(a_ref[0])
        bt = _swap_leading(b_ref[0])
        fa = f_ref[...]
        ys = []
        for f in range(f_block):
            x = jnp.concatenate([at[f], bt[f]], axis=0)
            ys.append(jnp.dot(fa, x, preferred_element_type=F32).astype(BF16))
        f0 = pl.multiple_of(j * f_block, f_block)
        y_ref[:, pl.ds(f0, f_block), :] = _swap_leading(jnp.stack(ys, axis=0))

    @pl.when(j >= n_a_steps)
    def _():
        r0 = pl.multiple_of((j - n_a_steps) * (2 * k1_block), 2 * k1_block)
        rs = []
        for i in range(k1_block):
            y = y_ref[pl.ds(r0 + 2 * i, 2)].reshape(2 * n2, D_FOURIER)
            rs.append(jnp.dot(g_ref[i], y, preferred_element_type=F32).astype(BF16))
        o_ref[0] = _swap_leading(jnp.stack(rs, axis=0))


def _build_bias_table(w_ref, tab_ref):
    jc = lax.broadcasted_iota(jnp.int32, (GRID_W, LANES), 0)
    lane = lax.broadcasted_iota(jnp.int32, (GRID_W, LANES), 1)
    c = lane & (GRID_W - 1)
    cstart = jnp.clip(c - WIN_COLS // 2, 0, GRID_W - WIN_COLS)
    col_ok = (jc >= cstart) & (jc < cstart + WIN_COLS)
    first = lane < GRID_W
    neg = jnp.full((GRID_W, LANES), NEG_BIG, F32)
    for h in range(HEADS_PER_BLOCK):
        for wtype in range(N_WIN_TYPES):
            rel = _WINDOW_PLAN[wtype]
            for jr in range(KEY_ROWS):
                ok = [rel[r] <= jr < rel[r] + WIN_ROWS for r in range(PAIR_ROWS)]
                d = jr - PAIR_ROWS * wtype + (WIN_ROWS - 1)
                rows = pl.ds(jr * GRID_W, GRID_W)
                if not any(ok):
                    tab_ref[h, wtype, rows, :] = neg
                    continue
                assert 0 <= d < BIAS_ROWS and (not ok[0] or d < RPB_ROWS) and (not ok[1] or d >= 1)
                x = jnp.broadcast_to(w_ref[0, h, d:d + 1, :], (GRID_W, LANES))
                x = pltpu.roll(x, 0, 1, stride=1, stride_axis=0)
                if all(ok):
                    valid = col_ok
                else:
                    valid = col_ok & (first if ok[0] else jnp.logical_not(first))
                tab_ref[h, wtype, rows, :] = jnp.where(valid, x * LOG2E, neg)


def _natten_kernel(q_ref, k_ref, v_ref, w_ref, o_ref, tab_ref, va_ref, vb_ref, *, rows, pairs):
    @pl.when((pl.program_id(1) == 0) & (pl.program_id(2) == 0))
    def _():
        _build_bias_table(w_ref, tab_ref)

    @pl.when(pl.program_id(2) == 0)
    def _():
        v = v_ref[0]
        first_v = lax.broadcasted_iota(jnp.int32, v.shape, 1) < HEAD_DIM
        one = jnp.ones_like(v)
        va_ref[...] = jnp.where(first_v, v, one)
        vb_ref[...] = jnp.where(first_v, one, v)

    mblk = pl.program_id(2)
    lane = lax.broadcasted_iota(jnp.int32, (PAIR_Q, LANES), 1)
    first = lane < HEAD_DIM
    vrefs = (va_ref, vb_ref)

    for p in range(pairs):
        i0 = PAIR_ROWS * (mblk * pairs + p)
        ws = jnp.clip(i0 - WIN_ROWS // 2, 0, rows - KEY_ROWS)
        wtype = lax.shift_right_logical(i0 - ws, 1)
        kstart = pl.multiple_of(ws * GRID_W, LANES)
        kw = k_ref[0, pl.ds(kstart, KEY_N), :]
        qp = q_ref[0, p * PAIR_Q:(p + 1) * PAIR_Q, :].astype(F32)
        outs = []
        for h in range(HEADS_PER_BLOCK):
            qm = jnp.where(first if h == 0 else jnp.logical_not(first), qp, 0.0).astype(BF16)
            s = lax.dot_general(kw, qm, (((1,), (1,)), ((), ())), preferred_element_type=F32)
            s = s + tab_ref[h, wtype]
            mx = jnp.max(s, axis=0, keepdims=True)
            pe = jnp.exp2(s - mx).astype(BF16)
            vw = vrefs[h][pl.ds(kstart, KEY_N), :]
            outs.append(lax.dot_general(pe, vw, (((0,), (0,)), ((), ())), preferred_element_type=F32))
        num = jnp.where(first, outs[0], outs[1])
        den = pltpu.roll(jnp.where(first, outs[1], outs[0]), HEAD_DIM, 1)
        o_ref[0, p * PAIR_Q:(p + 1) * PAIR_Q, :] = (num * (1.0 / den)).astype(BF16)


def _outffn_kernel(x_ref, f_ref, a_ref, mod_ref, gf_ref, ga_ref, wf_ref, wo_ref, gffn_ref,
                   wg_ref, wu_ref, wd_ref, o_ref):
    m = mod_ref[0]
    for lo in range(0, x_ref.shape[1], OUTFFN_ROWS):
        rows = slice(lo, lo + OUTFFN_ROWS)
        x = x_ref[0, rows, :]
        f = f_ref[0, rows, :]
        fo = jnp.concatenate(
            [jnp.dot(f[:, g * FG_DIM:(g + 1) * FG_DIM], wf_ref[g], preferred_element_type=F32)
             for g in range(N_FGROUPS)], axis=-1)
        fn = _rms(fo) * gf_ref[...]
        an = _rms(a_ref[0, rows, :].astype(F32)) * ga_ref[...]
        cat = jnp.concatenate([fn, an], axis=-1).astype(BF16)
        mix = jnp.dot(cat, wo_ref[...], preferred_element_type=F32)
        x1 = x + m[2:3] * mix
        h2 = (_rms(x1) * gffn_ref[...] * (1.0 + m[4:5]) + m[3:4]).astype(BF16)
        gate = jnp.dot(h2, wg_ref[...], preferred_element_type=F32)
        up = jnp.dot(h2, wu_ref[...], preferred_element_type=F32)
        act = (gate * jax.nn.sigmoid(gate) * up).astype(BF16)
        ff = jnp.dot(act, wd_ref[...], preferred_element_type=F32)
        o_ref[0, rows, :] = x1 + m[5:6] * ff


def _bf16_const(a):
    return jnp.asarray(a, F32).astype(BF16)


def _const_spec(shape):
    nd = len(shape)
    return pl.BlockSpec(shape, lambda *_: (0,) * nd, pipeline_mode=pl.Buffered(1))


def _adaln(c_all, w_ada, b_ada):
    n = c_all.shape[0]
    tn = 1024
    return pl.pallas_call(
        _adaln_kernel,
        grid=(N_MOD * D_MODEL // tn,),
        in_specs=[pl.BlockSpec((n, D_MODEL), lambda j: (0, 0)),
                  pl.BlockSpec((D_MODEL, tn), lambda j: (0, j)),
                  pl.BlockSpec((1, tn), lambda j: (0, j))],
        out_specs=pl.BlockSpec((n, tn), lambda j: (0, j)),
        out_shape=jax.ShapeDtypeStruct((n, N_MOD * D_MODEL), F32),
        compiler_params=_cparams(1),
        name="adaln",
    )(c_all, w_ada, b_ada.reshape(1, -1))


def _inproj(x, mod, g_attn, w_in, cs, pm, gq, gk, tm):
    b, t, _ = x.shape
    tok = lambda d: pl.BlockSpec((1, tm, d), lambda i, j: (i, j, 0))
    out = jax.ShapeDtypeStruct((b, t, D_FOURIER), BF16)
    return pl.pallas_call(
        _inproj_kernel,
        grid=(b, t // tm),
        in_specs=[tok(D_MODEL),
                  pl.BlockSpec((1, 8, D_MODEL), lambda i, j: (i, 0, 0)),
                  _const_spec((1, D_MODEL)), _const_spec((D_MODEL, D_IN)),
                  _const_spec((FG_DIM, 2 * FG_DIM)), _const_spec((HEAD_MEAN_W, HEAD_MEAN_W)),
                  _const_spec((1, D_NA)), _const_spec((1, D_NA))],
        out_specs=[tok(D_FOURIER)] * 5,
        out_shape=[out] * 5,
        compiler_params=_cparams(2),
        name="inproj",
    )(x, mod, g_attn, w_in, cs, pm, gq, gk)


def _fourier(ua, ub, n1, n2):
    b, t, d = ua.shape
    assert n1 * n2 == t and n2 % BF16_SUBLANES == 0 and n1 % BF16_SUBLANES == 0
    fa = _bf16_const(_stage_a_matrix(n1))
    gm = _bf16_const(_stage_b_matrices(n1, n2))
    f_block = k1_block = BF16_SUBLANES
    n_a = n2 // f_block
    n_b = n1 // k1_block
    in_spec = pl.BlockSpec((1, n1, f_block, d), lambda i, j: (i, 0, jnp.minimum(j, n_a - 1), 0))
    out = pl.pallas_call(
        functools.partial(_dft_kernel, n_a_steps=n_a, f_block=f_block, k1_block=k1_block, n2=n2),
        grid=(b, n_a + n_b),
        in_specs=[in_spec, in_spec, _const_spec((2 * n1, 2 * n1)),
                  pl.BlockSpec((k1_block, n2, 2 * n2), lambda i, j: (jnp.maximum(j - n_a, 0), 0, 0))],
        out_specs=pl.BlockSpec((1, n2, k1_block, d), lambda i, j: (i, 0, jnp.maximum(j - n_a, 0), 0)),
        out_shape=jax.ShapeDtypeStruct((b, n2, n1, d), BF16),
        scratch_shapes=[pltpu.VMEM((2 * n1, n2, d), BF16)],
        compiler_params=_cparams(2),
        name="dft",
    )(ua.reshape(b, n1, n2, d), ub.reshape(b, n1, n2, d), fa, gm)
    return out.reshape(b, t, d)


def _natten(q, k, v, bias_rows):
    b, t, _ = q.shape
    rows = t // GRID_W
    pairs = 16
    tq = pairs * PAIR_Q
    return pl.pallas_call(
        functools.partial(_natten_kernel, rows=rows, pairs=pairs),
        grid=(N_HEAD_BLOCKS, b, t // tq),
        in_specs=[pl.BlockSpec((1, tq, LANES), lambda hb, i, j: (i, j, hb)),
                  pl.BlockSpec((1, t, LANES), lambda hb, i, j: (i, 0, hb)),
                  pl.BlockSpec((1, t, LANES), lambda hb, i, j: (i, 0, hb)),
                  pl.BlockSpec((1, HEADS_PER_BLOCK, BIAS_ROWS, LANES), lambda hb, i, j: (hb, 0, 0, 0))],
        out_specs=pl.BlockSpec((1, tq, LANES), lambda hb, i, j: (i, j, hb)),
        out_shape=jax.ShapeDtypeStruct((b, t, D_NA), BF16),
        scratch_shapes=[pltpu.VMEM((HEADS_PER_BLOCK, N_WIN_TYPES, KEY_N, PAIR_Q), F32),
                        pltpu.VMEM((t, LANES), BF16), pltpu.VMEM((t, LANES), BF16)],
        compiler_params=_cparams(3),
        name="natten",
    )(q, k, v, bias_rows)


def _outffn(x, f, a, mod, gf, ga, wf, wo, gffn, wg, wu, wd, tm):
    b, t, _ = x.shape
    tok = lambda d: pl.BlockSpec((1, tm, d), lambda i, j: (i, j, 0))
    return pl.pallas_call(
        _outffn_kernel,
        grid=(b, t // tm),
        in_specs=[tok(D_MODEL), tok(D_FOURIER), tok(D_NA),
                  pl.BlockSpec((1, 8, D_MODEL), lambda i, j: (i, 0, 0)),
                  _const_spec((1, D_FOURIER)), _const_spec((1, D_NA)),
                  _const_spec((N_FGROUPS, FG_DIM, FG_DIM)), _const_spec((D_MODEL, D_MODEL)),
                  _const_spec((1, D_MODEL)), _const_spec((D_MODEL, D_FF)),
                  _const_spec((D_MODEL, D_FF)), _const_spec((D_FF, D_MODEL))],
        out_specs=tok(D_MODEL),
        out_shape=jax.ShapeDtypeStruct((b, t, D_MODEL), F32),
        compiler_params=_cparams(2),
        name="outffn",
    )(x, f, a, mod, gf, ga, wf, wo, gffn, wg, wu, wd)


def _stage_bias_rows(rpb):
    rev = rpb[:, :, ::-1]
    r0 = jnp.pad(rev, ((0, 0), (0, 1), (0, 0)))
    r1 = jnp.pad(rev, ((0, 0), (1, 0), (0, 0)))
    half = WIN_COLS - 1
    gap = GRID_W - RPB_COLS
    z = jnp.zeros((N_HEADS, BIAS_ROWS, gap), rpb.dtype)
    w = jnp.concatenate([r0[..., half:], z, r1, z, r0[..., :half]], axis=-1)
    assert w.shape == (N_HEADS, BIAS_ROWS, LANES)
    return w.reshape(N_HEAD_BLOCKS, HEADS_PER_BLOCK, BIAS_ROWS, LANES)


def kernel(x_prompt, x_sample, c_prompt, c_sample, w_ada, b_ada, g_attn, w_in, g_q, g_k, w_fmix, rpb,
           g_fout, g_aout, w_o, g_ffn, w_gate, w_up, w_down):
    assert w_ada.shape[0] == 1
    nb_p, nb_s = c_prompt.shape[0], c_sample.shape[0]
    n_c = nb_p + nb_s
    c_all = jnp.concatenate([c_prompt, c_sample, jnp.zeros((-n_c % 8, D_MODEL), F32)], axis=0)
    mod = _adaln(c_all, w_ada[0], b_ada[0])[:n_c].reshape(n_c, N_MOD, D_MODEL)
    mod = jnp.pad(mod, ((0, 0), (0, 8 - N_MOD), (0, 0)))

    w_in_b = w_in[0].astype(BF16)
    w_o_b = w_o[0].astype(BF16)
    w_g_b = w_gate[0].astype(BF16)
    w_u_b = w_up[0].astype(BF16)
    w_d_b = w_down[0].astype(BF16)
    w_f_b = w_fmix[0].astype(BF16)
    cs = _bf16_const(_channel_dft_matrix())
    pm = _bf16_const(_head_mean_matrix())
    gq = jnp.tile(g_q[0], N_HEADS).reshape(1, D_NA)
    gk = jnp.tile(g_k[0], N_HEADS).reshape(1, D_NA)
    row = lambda v: v.reshape(1, -1)
    bias_rows = _stage_bias_rows(rpb[0])

    def trunk(x, mod_g):
        b, t, _ = x.shape
        ua, ub, q, k, v = _inproj(x, mod_g, row(g_attn[0]), w_in_b, cs, pm, gq, gk, tm=1024)
        f = _fourier(ua, ub, t // GRID_W, GRID_W)
        a = _natten(q, k, v, bias_rows)
        return _outffn(x, f, a, mod_g, row(g_fout[0]), row(g_aout[0]), w_f_b, w_o_b, row(g_ffn[0]),
                       w_g_b, w_u_b, w_d_b, tm=1024)

    return trunk(x_prompt, mod[:nb_p]), trunk(x_sample, mod[nb_p:])
```

```python
import functools

import numpy as np
import jax
import jax.numpy as jnp
from jax import lax
from jax.experimental import pallas as pl
from jax.experimental.pallas import tpu as pltpu

F32 = jnp.float32
BF16 = jnp.bfloat16

D_MODEL = 1024
GRID_W = 64
D_FOURIER = 512
N_FGROUPS = 4
FG_DIM = 128
D_NA = 512
N_HEADS = 8
HEAD_DIM = 64
WIN_ROWS = 8
WIN_COLS = 16
RPB_ROWS = 2 * WIN_ROWS - 1
RPB_COLS = 2 * WIN_COLS - 1
D_IN = 2048
D_FF = 2816
N_MOD = 6
EPS = 1e-6

LANES = 128
HEAD_MEAN_W = 256
BF16_SUBLANES = 16
HEADS_PER_BLOCK = LANES // HEAD_DIM
N_HEAD_BLOCKS = N_HEADS // HEADS_PER_BLOCK
PAIR_ROWS = 2
PAIR_Q = PAIR_ROWS * GRID_W
KEY_ROWS = 10
KEY_N = KEY_ROWS * GRID_W
N_WIN_TYPES = 5
BIAS_ROWS = RPB_ROWS + 1
NEG_BIG = -1e30
LOG2E = 1.4426950408889634
OUTFFN_ROWS = 1024

VMEM_LIMIT = 56 * 1024 * 1024


def _cparams(n_axes):
    return pltpu.CompilerParams(
        dimension_semantics=("arbitrary",) * n_axes, vmem_limit_bytes=VMEM_LIMIT)


def _channel_dft_matrix():
    c = np.arange(FG_DIM)
    ang = 2.0 * np.pi * ((c[:, None] * c[None, :]) % FG_DIM) / FG_DIM
    s = 1.0 / np.sqrt(FG_DIM)
    return np.concatenate([np.cos(ang) * s, np.sin(ang) * s], axis=1)


def _stage_a_matrix(n1):
    k = np.arange(n1)
    ang = 2.0 * np.pi * ((k[:, None] * k[None, :]) % n1) / n1
    c, s = np.cos(ang), np.sin(ang)
    re = np.concatenate([c, -s], axis=1)
    im = np.concatenate([-s, -c], axis=1)
    m = np.stack([re, im], axis=1).reshape(2 * n1, 2 * n1)
    return m / np.sqrt(n1)


def _stage_b_matrices(n1, n2):
    t = n1 * n2
    k1 = np.arange(n1)[:, None, None]
    k2 = np.arange(n2)[None, :, None]
    m = np.arange(n2)[None, None, :]
    idx = (m * k2 * n1 + m * k1) % t
    ang = 2.0 * np.pi * idx / t
    g = np.concatenate([np.cos(ang), np.sin(ang)], axis=2)
    return g / np.sqrt(n2)


def _head_mean_matrix():
    h = np.arange(HEAD_MEAN_W) // HEAD_DIM
    return (h[:, None] == h[None, :]).astype(np.float64) / HEAD_DIM


def _window_plan(rows):
    types = {}
    for m in range(rows // PAIR_ROWS):
        i0 = PAIR_ROWS * m
        ws = min(max(i0 - WIN_ROWS // 2, 0), rows - KEY_ROWS)
        e2 = (i0 - ws) // 2
        rel = tuple(min(max(i0 + r - WIN_ROWS // 2, 0), rows - WIN_ROWS) - ws for r in range(PAIR_ROWS))
        assert types.setdefault(e2, rel) == rel
        assert (i0 - ws) % 2 == 0 and all(0 <= x and x + WIN_ROWS <= KEY_ROWS for x in rel)
    assert sorted(types) == list(range(N_WIN_TYPES))
    return types


_WINDOW_PLAN = _window_plan(64)
assert _WINDOW_PLAN == _window_plan(128)


def _split_bf16(x):
    hi = x.astype(BF16)
    return hi, (x - hi.astype(F32)).astype(BF16)


def _adaln_kernel(c_ref, w_ref, b_ref, o_ref):
    c = c_ref[...]
    s_hi, s_lo = _split_bf16(c * jax.nn.sigmoid(c))
    w_hi, w_lo = _split_bf16(w_ref[...])
    dot = functools.partial(jnp.dot, preferred_element_type=F32)
    o_ref[...] = dot(s_hi, w_hi) + (dot(s_hi, w_lo) + dot(s_lo, w_hi)) + b_ref[...]


def _rms(x):
    return x * lax.rsqrt(jnp.mean(x * x, axis=-1, keepdims=True) + EPS)


def _head_rms(t, p):
    sq = (t * t).astype(BF16)
    ms = jnp.concatenate(
        [jnp.dot(sq[:, c:c + HEAD_MEAN_W], p, preferred_element_type=F32) for c in range(0, D_NA, HEAD_MEAN_W)],
        axis=-1)
    return t * lax.rsqrt(ms + EPS)


def _inproj_kernel(x_ref, mod_ref, gattn_ref, win_ref, cs_ref, pm_ref, gq_ref, gk_ref,
                   ua_ref, ub_ref, q_ref, k_ref, v_ref):
    x = x_ref[0]
    m = mod_ref[0]
    h = (_rms(x) * gattn_ref[...] * (1.0 + m[1:2]) + m[0:1]).astype(BF16)
    z = jnp.dot(h, win_ref[...], preferred_element_type=F32)
    cs = cs_ref[...]
    for g in range(N_FGROUPS):
        ug = z[:, g * FG_DIM:(g + 1) * FG_DIM].astype(BF16)
        ab = jnp.dot(ug, cs, preferred_element_type=F32)
        ua_ref[0, :, g * FG_DIM:(g + 1) * FG_DIM] = ab[:, :FG_DIM].astype(BF16)
        ub_ref[0, :, g * FG_DIM:(g + 1) * FG_DIM] = ab[:, FG_DIM:].astype(BF16)
    p = pm_ref[...]
    q = z[:, D_FOURIER:D_FOURIER + D_NA]
    k = z[:, D_FOURIER + D_NA:D_FOURIER + 2 * D_NA]
    q_ref[0] = (_head_rms(q, p) * gq_ref[...] * (HEAD_DIM ** -0.5 * LOG2E)).astype(BF16)
    k_ref[0] = (_head_rms(k, p) * gk_ref[...]).astype(BF16)
    v_ref[0] = z[:, D_FOURIER + 2 * D_NA:].astype(BF16)


def _swap_leading(x):
    return jnp.swapaxes(x, 0, 1)


def _dft_kernel(a_ref, b_ref, f_ref, g_ref, o_ref, y_ref, *, n_a_steps, f_block, k1_block, n2):
    j = pl.program_id(1)

    @pl.when(j < n_a_steps)
    def _():
        at = _swap_leading(a_ref[0])
        bt = _swap_leading(b_ref[0])
        fa = f_ref[...]
        ys = []
        for f in range(f_block):
            x = jnp.concatenate([at[f], bt[f]], axis=0)
            ys.append(jnp.dot(fa, x, preferred_element_type=F32).astype(BF16))
        f0 = pl.multiple_of(j * f_block, f_block)
        y_ref[:, pl.ds(f0, f_block), :] = _swap_leading(jnp.stack(ys, axis=0))

    @pl.when(j >= n_a_steps)
    def _():
        r0 = pl.multiple_of((j - n_a_steps) * (2 * k1_block), 2 * k1_block)
        rs = []
        for i in range(k1_block):
            y = y_ref[pl.ds(r0 + 2 * i, 2)].reshape(2 * n2, D_FOURIER)
            rs.append(jnp.dot(g_ref[i], y, preferred_element_type=F32).astype(BF16))
        o_ref[0] = _swap_leading(jnp.stack(rs, axis=0))


def _build_bias_table(w_ref, tab_ref):
    jc = lax.broadcasted_iota(jnp.int32, (GRID_W, LANES), 0)
    lane = lax.broadcasted_iota(jnp.int32, (GRID_W, LANES), 1)
    c = lane & (GRID_W - 1)
    cstart = jnp.clip(c - WIN_COLS // 2, 0, GRID_W - WIN_COLS)
    col_ok = (jc >= cstart) & (jc < cstart + WIN_COLS)
    first = lane < GRID_W
    neg = jnp.full((GRID_W, LANES), NEG_BIG, F32)
    for h in range(HEADS_PER_BLOCK):
        for wtype in range(N_WIN_TYPES):
            rel = _WINDOW_PLAN[wtype]
            for jr in range(KEY_ROWS):
                ok = [rel[r] <= jr < rel[r] + WIN_ROWS for r in range(PAIR_ROWS)]
                d = jr - PAIR_ROWS * wtype + (WIN_ROWS - 1)
                rows = pl.ds(jr * GRID_W, GRID_W)
                if not any(ok):
                    tab_ref[h, wtype, rows, :] = neg
                    continue
                assert 0 <= d < BIAS_ROWS and (not ok[0] or d < RPB_ROWS) and (not ok[1] or d >= 1)
                x = jnp.broadcast_to(w_ref[0, h, d:d + 1, :], (GRID_W, LANES))
                x = pltpu.roll(x, 0, 1, stride=1, stride_axis=0)
                if all(ok):
                    valid = col_ok
                else:
                    valid = col_ok & (first if ok[0] else jnp.logical_not(first))
                tab_ref[h, wtype, rows, :] = jnp.where(valid, x * LOG2E, neg)


def _natten_kernel(q_ref, k_ref, v_ref, w_ref, o_ref, tab_ref, va_ref, vb_ref, *, rows, pairs):
    @pl.when((pl.program_id(1) == 0) & (pl.program_id(2) == 0))
    def _():
        _build_bias_table(w_ref, tab_ref)

    @pl.when(pl.program_id(2) == 0)
    def _():
        v = v_ref[0]
        first_v = lax.broadcasted_iota(jnp.int32, v.shape, 1) < HEAD_DIM
        one = jnp.ones_like(v)
        va_ref[...] = jnp.where(first_v, v, one)
        vb_ref[...] = jnp.where(first_v, one, v)

    mblk = pl.program_id(2)
    lane = lax.broadcasted_iota(jnp.int32, (PAIR_Q, LANES), 1)
    first = lane < HEAD_DIM
    vrefs = (va_ref, vb_ref)

    for p in range(pairs):
        i0 = PAIR_ROWS * (mblk * pairs + p)
        ws = jnp.clip(i0 - WIN_ROWS // 2, 0, rows - KEY_ROWS)
        wtype = lax.shift_right_logical(i0 - ws, 1)
        kstart = pl.multiple_of(ws * GRID_W, LANES)
        kw = k_ref[0, pl.ds(kstart, KEY_N), :]
        qp = q_ref[0, p * PAIR_Q:(p + 1) * PAIR_Q, :].astype(F32)
        outs = []
        for h in range(HEADS_PER_BLOCK):
            qm = jnp.where(first if h == 0 else jnp.logical_not(first), qp, 0.0).astype(BF16)
            s = lax.dot_general(kw, qm, (((1,), (1,)), ((), ())), preferred_element_type=F32)
            s = s + tab_ref[h, wtype]
            mx = jnp.max(s, axis=0, keepdims=True)
            pe = jnp.exp2(s - mx).astype(BF16)
            vw = vrefs[h][pl.ds(kstart, KEY_N), :]
            outs.append(lax.dot_general(pe, vw, (((0,), (0,)), ((), ())), preferred_element_type=F32))
        num = jnp.where(first, outs[0], outs[1])
        den = pltpu.roll(jnp.where(first, outs[1], outs[0]), HEAD_DIM, 1)
        o_ref[0, p * PAIR_Q:(p + 1) * PAIR_Q, :] = (num * (1.0 / den)).astype(BF16)


def _outffn_kernel(x_ref, f_ref, a_ref, mod_ref, gf_ref, ga_ref, wf_ref, wo_ref, gffn_ref,
                   wg_ref, wu_ref, wd_ref, o_ref):
    m = mod_ref[0]
    for lo in range(0, x_ref.shape[1], OUTFFN_ROWS):
        rows = slice(lo, lo + OUTFFN_ROWS)
        x = x_ref[0, rows, :]
        f = f_ref[0, rows, :]
        fo = jnp.concatenate(
            [jnp.dot(f[:, g * FG_DIM:(g + 1) * FG_DIM], wf_ref[g], preferred_element_type=F32)
             for g in range(N_FGROUPS)], axis=-1)
        fn = _rms(fo) * gf_ref[...]
        an = _rms(a_ref[0, rows, :].astype(F32)) * ga_ref[...]
        cat = jnp.concatenate([fn, an], axis=-1).astype(BF16)
        mix = jnp.dot(cat, wo_ref[...], preferred_element_type=F32)
        x1 = x + m[2:3] * mix
        h2 = (_rms(x1) * gffn_ref[...] * (1.0 + m[4:5]) + m[3:4]).astype(BF16)
        gate = jnp.dot(h2, wg_ref[...], preferred_element_type=F32)
        up = jnp.dot(h2, wu_ref[...], preferred_element_type=F32)
        act = (gate * jax.nn.sigmoid(gate) * up).astype(BF16)
        ff = jnp.dot(act, wd_ref[...], preferred_element_type=F32)
        o_ref[0, rows, :] = x1 + m[5:6] * ff


def _bf16_const(a):
    return jnp.asarray(a, F32).astype(BF16)


def _const_spec(shape):
    nd = len(shape)
    return pl.BlockSpec(shape, lambda *_: (0,) * nd, pipeline_mode=pl.Buffered(1))


def _adaln(c_all, w_ada, b_ada):
    n = c_all.shape[0]
    tn = 1024
    return pl.pallas_call(
        _adaln_kernel,
        grid=(N_MOD * D_MODEL // tn,),
        in_specs=[pl.BlockSpec((n, D_MODEL), lambda j: (0, 0)),
                  pl.BlockSpec((D_MODEL, tn), lambda j: (0, j)),
                  pl.BlockSpec((1, tn), lambda j: (0, j))],
        out_specs=pl.BlockSpec((n, tn), lambda j: (0, j)),
        out_shape=jax.ShapeDtypeStruct((n, N_MOD * D_MODEL), F32),
        compiler_params=_cparams(1),
        name="adaln",
    )(c_all, w_ada, b_ada.reshape(1, -1))


def _inproj(x, mod, g_attn, w_in, cs, pm, gq, gk, tm):
    b, t, _ = x.shape
    tok = lambda d: pl.BlockSpec((1, tm, d), lambda i, j: (i, j, 0))
    out = jax.ShapeDtypeStruct((b, t, D_FOURIER), BF16)
    return pl.pallas_call(
        _inproj_kernel,
        grid=(b, t // tm),
        in_specs=[tok(D_MODEL),
                  pl.BlockSpec((1, 8, D_MODEL), lambda i, j: (i, 0, 0)),
                  _const_spec((1, D_MODEL)), _const_spec((D_MODEL, D_IN)),
                  _const_spec((FG_DIM, 2 * FG_DIM)), _const_spec((HEAD_MEAN_W, HEAD_MEAN_W)),
                  _const_spec((1, D_NA)), _const_spec((1, D_NA))],
        out_specs=[tok(D_FOURIER)] * 5,
        out_shape=[out] * 5,
        compiler_params=_cparams(2),
        name="inproj",
    )(x, mod, g_attn, w_in, cs, pm, gq, gk)


def _fourier(ua, ub, n1, n2):
    b, t, d = ua.shape
    assert n1 * n2 == t and n2 % BF16_SUBLANES == 0 and n1 % BF16_SUBLANES == 0
    fa = _bf16_const(_stage_a_matrix(n1))
    gm = _bf16_const(_stage_b_matrices(n1, n2))
    f_block = k1_block = 2 * BF16_SUBLANES
    n_a = n2 // f_block
    n_b = n1 // k1_block
    in_spec = pl.BlockSpec((1, n1, f_block, d), lambda i, j: (i, 0, jnp.minimum(j, n_a - 1), 0))
    out = pl.pallas_call(
        functools.partial(_dft_kernel, n_a_steps=n_a, f_block=f_block, k1_block=k1_block, n2=n2),
        grid=(b, n_a + n_b),
        in_specs=[in_spec, in_spec, _const_spec((2 * n1, 2 * n1)),
                  pl.BlockSpec((k1_block, n2, 2 * n2), lambda i, j: (jnp.maximum(j - n_a, 0), 0, 0))],
        out_specs=pl.BlockSpec((1, n2, k1_block, d), lambda i, j: (i, 0, jnp.maximum(j - n_a, 0), 0)),
        out_shape=jax.ShapeDtypeStruct((b, n2, n1, d), BF16),
        scratch_shapes=[pltpu.VMEM((2 * n1, n2, d), BF16)],
        compiler_params=_cparams(2),
        name="dft",
    )(ua.reshape(b, n1, n2, d), ub.reshape(b, n1, n2, d), fa, gm)
    return out.reshape(b, t, d)


def _natten(q, k, v, bias_rows):
    b, t, _ = q.shape
    rows = t // GRID_W
    pairs = 32
    tq = pairs * PAIR_Q
    return pl.pallas_call(
        functools.partial(_natten_kernel, rows=rows, pairs=pairs),
        grid=(N_HEAD_BLOCKS, b, t // tq),
        in_specs=[pl.BlockSpec((1, tq, LANES), lambda hb, i, j: (i, j, hb)),
                  pl.BlockSpec((1, t, LANES), lambda hb, i, j: (i, 0, hb)),
                  pl.BlockSpec((1, t, LANES), lambda hb, i, j: (i, 0, hb)),
                  pl.BlockSpec((1, HEADS_PER_BLOCK, BIAS_ROWS, LANES), lambda hb, i, j: (hb, 0, 0, 0))],
        out_specs=pl.BlockSpec((1, tq, LANES), lambda hb, i, j: (i, j, hb)),
        out_shape=jax.ShapeDtypeStruct((b, t, D_NA), BF16),
        scratch_shapes=[pltpu.VMEM((HEADS_PER_BLOCK, N_WIN_TYPES, KEY_N, PAIR_Q), F32),
                        pltpu.VMEM((t, LANES), BF16), pltpu.VMEM((t, LANES), BF16)],
        compiler_params=_cparams(3),
        name="natten",
    )(q, k, v, bias_rows)


def _outffn(x, f, a, mod, gf, ga, wf, wo, gffn, wg, wu, wd, tm):
    b, t, _ = x.shape
    tok = lambda d: pl.BlockSpec((1, tm, d), lambda i, j: (i, j, 0))
    return pl.pallas_call(
        _outffn_kernel,
        grid=(b, t // tm),
        in_specs=[tok(D_MODEL), tok(D_FOURIER), tok(D_NA),
                  pl.BlockSpec((1, 8, D_MODEL), lambda i, j: (i, 0, 0)),
                  _const_spec((1, D_FOURIER)), _const_spec((1, D_NA)),
                  _const_spec((N_FGROUPS, FG_DIM, FG_DIM)), _const_spec((D_MODEL, D_MODEL)),
                  _const_spec((1, D_MODEL)), _const_spec((D_MODEL, D_FF)),
                  _const_spec((D_MODEL, D_FF)), _const_spec((D_FF, D_MODEL))],
        out_specs=tok(D_MODEL),
        out_shape=jax.ShapeDtypeStruct((b, t, D_MODEL), F32),
        compiler_params=_cparams(2),
        name="outffn",
    )(x, f, a, mod, gf, ga, wf, wo, gffn, wg, wu, wd)


def _stage_bias_rows(rpb):
    rev = rpb[:, :, ::-1]
    r0 = jnp.pad(rev, ((0, 0), (0, 1), (0, 0)))
    r1 = jnp.pad(rev, ((0, 0), (1, 0), (0, 0)))
    half = WIN_COLS - 1
    gap = GRID_W - RPB_COLS
    z = jnp.zeros((N_HEADS, BIAS_ROWS, gap), rpb.dtype)
    w = jnp.concatenate([r0[..., half:], z, r1, z, r0[..., :half]], axis=-1)
    assert w.shape == (N_HEADS, BIAS_ROWS, LANES)
    return w.reshape(N_HEAD_BLOCKS, HEADS_PER_BLOCK, BIAS_ROWS, LANES)


def kernel(x_prompt, x_sample, c_prompt, c_sample, w_ada, b_ada, g_attn, w_in, g_q, g_k, w_fmix, rpb,
           g_fout, g_aout, w_o, g_ffn, w_gate, w_up, w_down):
    assert w_ada.shape[0] == 1
    nb_p, nb_s = c_prompt.shape[0], c_sample.shape[0]
    n_c = nb_p + nb_s
    c_all = jnp.concatenate([c_prompt, c_sample, jnp.zeros((-n_c % 8, D_MODEL), F32)], axis=0)
    mod = _adaln(c_all, w_ada[0], b_ada[0])[:n_c].reshape(n_c, N_MOD, D_MODEL)
    mod = jnp.pad(mod, ((0, 0), (0, 8 - N_MOD), (0, 0)))

    w_in_b = w_in[0].astype(BF16)
    w_o_b = w_o[0].astype(BF16)
    w_g_b = w_gate[0].astype(BF16)
    w_u_b = w_up[0].astype(BF16)
    w_d_b = w_down[0].astype(BF16)
    w_f_b = w_fmix[0].astype(BF16)
    cs = _bf16_const(_channel_dft_matrix())
    pm = _bf16_const(_head_mean_matrix())
    gq = jnp.tile(g_q[0], N_HEADS).reshape(1, D_NA)
    gk = jnp.tile(g_k[0], N_HEADS).reshape(1, D_NA)
    row = lambda v: v.reshape(1, -1)
    bias_rows = _stage_bias_rows(rpb[0])

    def trunk(x, mod_g):
        b, t, _ = x.shape
        ua, ub, q, k, v = _inproj(x, mod_g, row(g_attn[0]), w_in_b, cs, pm, gq, gk, tm=1024)
        f = _fourier(ua, ub, t // GRID_W, GRID_W)
        a = _natten(q, k, v, bias_rows)
        return _outffn(x, f, a, mod_g, row(g_fout[0]), row(g_aout[0]), w_f_b, w_o_b, row(g_ffn[0]),
                       w_g_b, w_u_b, w_d_b, tm=1024)

    return trunk(x_prompt, mod[:nb_p]), trunk(x_sample, mod[nb_p:])
```

```python
import functools

import numpy as np
import jax
import jax.numpy as jnp
from jax import lax
from jax.experimental import pallas as pl
from jax.experimental.pallas import tpu as pltpu

F32 = jnp.float32
BF16 = jnp.bfloat16

D_MODEL = 1024
GRID_W = 64
D_FOURIER = 512
N_FGROUPS = 4
FG_DIM = 128
D_NA = 512
N_HEADS = 8
HEAD_DIM = 64
WIN_ROWS = 8
WIN_COLS = 16
RPB_ROWS = 2 * WIN_ROWS - 1
RPB_COLS = 2 * WIN_COLS - 1
D_IN = 2048
D_FF = 2816
N_MOD = 6
EPS = 1e-6

LANES = 128
HEAD_MEAN_W = 256
BF16_SUBLANES = 16
HEADS_PER_BLOCK = LANES // HEAD_DIM
N_HEAD_BLOCKS = N_HEADS // HEADS_PER_BLOCK
Q_ROWS = 8
Q_COLS = 16
UNIT_Q = Q_ROWS * Q_COLS
K_ROWS = 16
K_COLS = 32
UNIT_K = K_ROWS * K_COLS
N_COL_BLOCKS = GRID_W // Q_COLS
COL_STARTS = tuple(min(max(Q_COLS * j - WIN_COLS // 2, 0), GRID_W - K_COLS) for j in range(N_COL_BLOCKS))
ROW_TYPES = (0, WIN_ROWS // 2, WIN_ROWS)
BIAS_ROWS = RPB_ROWS + 1
NEG_BIG = -1e30
LOG2E = 1.4426950408889634
OUTFFN_ROWS = 1024

VMEM_LIMIT = 56 * 1024 * 1024


def _cparams(n_axes):
    return pltpu.CompilerParams(
        dimension_semantics=("arbitrary",) * n_axes, vmem_limit_bytes=VMEM_LIMIT)


def _channel_dft_matrix():
    c = np.arange(FG_DIM)
    ang = 2.0 * np.pi * ((c[:, None] * c[None, :]) % FG_DIM) / FG_DIM
    s = 1.0 / np.sqrt(FG_DIM)
    return np.concatenate([np.cos(ang) * s, np.sin(ang) * s], axis=1)


def _stage_a_matrix(n1):
    k = np.arange(n1)
    ang = 2.0 * np.pi * ((k[:, None] * k[None, :]) % n1) / n1
    c, s = np.cos(ang), np.sin(ang)
    re = np.concatenate([c, -s], axis=1)
    im = np.concatenate([-s, -c], axis=1)
    m = np.stack([re, im], axis=1).reshape(2 * n1, 2 * n1)
    return m / np.sqrt(n1)


def _stage_b_matrices(n1, n2):
    t = n1 * n2
    k1 = np.arange(n1)[:, None, None]
    k2 = np.arange(n2)[None, :, None]
    m = np.arange(n2)[None, None, :]
    idx = (m * k2 * n1 + m * k1) % t
    ang = 2.0 * np.pi * idx / t
    g = np.concatenate([np.cos(ang), np.sin(ang)], axis=2)
    return g / np.sqrt(n2)


def _head_mean_matrix():
    h = np.arange(HEAD_MEAN_W) // HEAD_DIM
    return (h[:, None] == h[None, :]).astype(np.float64) / HEAD_DIM


def _window_plan(rows):
    types = {}
    for blk in range(rows // Q_ROWS):
        i0 = Q_ROWS * blk
        ks = min(max(i0 - WIN_ROWS // 2, 0), rows - K_ROWS)
        rel = tuple(min(max(i0 + r - WIN_ROWS // 2, 0), rows - WIN_ROWS) - ks for r in range(Q_ROWS))
        assert all(0 <= x and x + WIN_ROWS <= K_ROWS for x in rel)
        assert types.setdefault(i0 - ks, rel) == rel
    assert sorted(types) == sorted(ROW_TYPES)
    return tuple(types[e] for e in ROW_TYPES)


_WINDOW_PLAN = _window_plan(64)
assert _WINDOW_PLAN == _window_plan(128)
for _j, _cs in enumerate(COL_STARTS):
    for _c in range(Q_COLS * _j, Q_COLS * (_j + 1)):
        _w0 = min(max(_c - WIN_COLS // 2, 0), GRID_W - WIN_COLS)
        assert _cs <= _w0 and _w0 + WIN_COLS <= _cs + K_COLS


def _split_bf16(x):
    hi = x.astype(BF16)
    return hi, (x - hi.astype(F32)).astype(BF16)


def _adaln_kernel(c_ref, w_ref, b_ref, o_ref):
    c = c_ref[...]
    s_hi, s_lo = _split_bf16(c * jax.nn.sigmoid(c))
    w_hi, w_lo = _split_bf16(w_ref[...])
    dot = functools.partial(jnp.dot, preferred_element_type=F32)
    o_ref[...] = dot(s_hi, w_hi) + (dot(s_hi, w_lo) + dot(s_lo, w_hi)) + b_ref[...]


def _rms(x):
    return x * lax.rsqrt(jnp.mean(x * x, axis=-1, keepdims=True) + EPS)


def _head_rms(t, p):
    sq = (t * t).astype(BF16)
    ms = jnp.concatenate(
        [jnp.dot(sq[:, c:c + HEAD_MEAN_W], p, preferred_element_type=F32) for c in range(0, D_NA, HEAD_MEAN_W)],
        axis=-1)
    return t * lax.rsqrt(ms + EPS)


def _inproj_kernel(x_ref, mod_ref, gattn_ref, win_ref, cs_ref, pm_ref, gq_ref, gk_ref,
                   ua_ref, ub_ref, q_ref, kc_ref, vc_ref):
    x = x_ref[0]
    m = mod_ref[0]
    h = (_rms(x) * gattn_ref[...] * (1.0 + m[1:2]) + m[0:1]).astype(BF16)
    z = jnp.dot(h, win_ref[...], preferred_element_type=F32)
    cs = cs_ref[...]
    for g in range(N_FGROUPS):
        ug = z[:, g * FG_DIM:(g + 1) * FG_DIM].astype(BF16)
        ab = jnp.dot(ug, cs, preferred_element_type=F32)
        ua_ref[0, :, g * FG_DIM:(g + 1) * FG_DIM] = ab[:, :FG_DIM].astype(BF16)
        ub_ref[0, :, g * FG_DIM:(g + 1) * FG_DIM] = ab[:, FG_DIM:].astype(BF16)
    p = pm_ref[...]
    q = z[:, D_FOURIER:D_FOURIER + D_NA]
    k = z[:, D_FOURIER + D_NA:D_FOURIER + 2 * D_NA]
    q_ref[0] = (_head_rms(q, p) * gq_ref[...] * (HEAD_DIM ** -0.5 * LOG2E)).astype(BF16)
    kn = _head_rms(k, p) * gk_ref[...]
    v = z[:, D_FOURIER + 2 * D_NA:]
    for src, dst in ((kn, kc_ref), (v, vc_ref)):
        grid = src.reshape(-1, GRID_W, D_NA)
        for j, c0 in enumerate(COL_STARTS):
            dst[0, j] = grid[:, c0:c0 + K_COLS, :].reshape(-1, D_NA).astype(BF16)


def _swap_leading(x):
    return jnp.swapaxes(x, 0, 1)


def _dft_kernel(a_ref, b_ref, f_ref, g_ref, o_ref, y_ref, *, n_a_steps, f_block, k1_block, n2):
    j = pl.program_id(1)

    @pl.when(j < n_a_steps)
    def _():
        at = _swap_leading(a_ref[0])
        bt = _swap_leading(b_ref[0])
        fa = f_ref[...]
        ys = []
        for f in range(f_block):
            x = jnp.concatenate([at[f], bt[f]], axis=0)
            ys.append(jnp.dot(fa, x, preferred_element_type=F32).astype(BF16))
        f0 = pl.multiple_of(j * f_block, f_block)
        y_ref[:, pl.ds(f0, f_block), :] = _swap_leading(jnp.stack(ys, axis=0))

    @pl.when(j >= n_a_steps)
    def _():
        r0 = pl.multiple_of((j - n_a_steps) * (2 * k1_block), 2 * k1_block)
        rs = []
        for i in range(k1_block):
            y = y_ref[pl.ds(r0 + 2 * i, 2)].reshape(2 * n2, D_FOURIER)
            rs.append(jnp.dot(g_ref[i], y, preferred_element_type=F32).astype(BF16))
        o_ref[0] = _swap_leading(jnp.stack(rs, axis=0))


def _build_bias_table(w_ref, tab_ref):
    jc = lax.broadcasted_iota(jnp.int32, (K_COLS, LANES), 0)
    lane = lax.broadcasted_iota(jnp.int32, (K_COLS, LANES), 1)
    c = lane & (Q_COLS - 1)
    neg = jnp.full((K_COLS, LANES), NEG_BIG, F32)
    for j, c0 in enumerate(COL_STARTS):
        wstart = jnp.clip(Q_COLS * j + c - WIN_COLS // 2, 0, GRID_W - WIN_COLS)
        col_ok = (c0 + jc >= wstart) & (c0 + jc < wstart + WIN_COLS)
        o = c0 - Q_COLS * j + WIN_COLS - 1
        in_row = [col_ok & (lane >= Q_COLS * r) & (lane < Q_COLS * (r + 1)) for r in range(Q_ROWS)]
        for h in range(HEADS_PER_BLOCK):
            rolled = {}
            for t, rel in enumerate(_WINDOW_PLAN):
                for jr in range(K_ROWS):
                    slab = neg
                    for r in range(Q_ROWS):
                        if not rel[r] <= jr < rel[r] + WIN_ROWS:
                            continue
                        dr = jr - ROW_TYPES[t] - r + WIN_ROWS - 1
                        assert 0 <= dr < RPB_ROWS
                        if (r, dr) not in rolled:
                            x = jnp.broadcast_to(w_ref[0, h, dr:dr + 1, :], (K_COLS, LANES))
                            base = (Q_COLS * r + o - (RPB_COLS - 1)) % LANES
                            rolled[r, dr] = pltpu.roll(x, base, 1, stride=1, stride_axis=0) * LOG2E
                        slab = jnp.where(in_row[r], rolled[r, dr], slab)
                    tab_ref[h, t, j, pl.ds(jr * K_COLS, K_COLS), :] = slab


def _natten_kernel(q_ref, kc_ref, vc_ref, w_ref, o_ref, tab_ref, *, rows, row_blocks):
    @pl.when((pl.program_id(1) == 0) & (pl.program_id(2) == 0))
    def _():
        _build_bias_table(w_ref, tab_ref)

    lane = lax.broadcasted_iota(jnp.int32, (UNIT_Q, LANES), 1)
    first = lane < HEAD_DIM
    first_k = lax.broadcasted_iota(jnp.int32, (UNIT_K, LANES), 1) < HEAD_DIM
    one = jnp.ones((UNIT_K, LANES), BF16)

    for rb in range(row_blocks):
        i0 = Q_ROWS * (pl.program_id(2) * row_blocks + rb)
        ks = jnp.clip(i0 - WIN_ROWS // 2, 0, rows - K_ROWS)
        rtype = lax.shift_right_logical(i0 - ks, 2)
        kstart = pl.multiple_of(ks * K_COLS, K_COLS)
        tok = slice(rb * Q_ROWS * GRID_W, (rb + 1) * Q_ROWS * GRID_W)
        qb = q_ref[0, tok, :].reshape(Q_ROWS, GRID_W, LANES)
        for j in range(N_COL_BLOCKS):
            cols = slice(j * Q_COLS, (j + 1) * Q_COLS)
            kw = kc_ref[0, j, pl.ds(kstart, UNIT_K), :]
            vwin = vc_ref[0, j, pl.ds(kstart, UNIT_K), :]
            qp = qb[:, cols, :].reshape(UNIT_Q, LANES).astype(F32)
            outs = []
            for h in range(HEADS_PER_BLOCK):
                qm = jnp.where(first if h == 0 else jnp.logical_not(first), qp, 0.0).astype(BF16)
                s = lax.dot_general(kw, qm, (((1,), (1,)), ((), ())), preferred_element_type=F32)
                s = s + tab_ref[h, rtype, j]
                mx = jnp.max(s, axis=0, keepdims=True)
                pe = jnp.exp2(s - mx).astype(BF16)
                vw = jnp.where(first_k, vwin, one) if h == 0 else jnp.where(first_k, one, vwin)
                outs.append(lax.dot_general(pe, vw, (((0,), (0,)), ((), ())), preferred_element_type=F32))
            num = jnp.where(first, outs[0], outs[1])
            den = pltpu.roll(jnp.where(first, outs[1], outs[0]), HEAD_DIM, 1)
            out = (num * (1.0 / den)).astype(BF16).reshape(Q_ROWS, Q_COLS, LANES)
            for r in range(Q_ROWS):
                o_ref[0, pl.ds(rb * Q_ROWS * GRID_W + r * GRID_W + j * Q_COLS, Q_COLS), :] = out[r]


def _outffn_kernel(x_ref, f_ref, a_ref, mod_ref, gf_ref, ga_ref, wf_ref, wo_ref, gffn_ref,
                   wg_ref, wu_ref, wd_ref, o_ref):
    m = mod_ref[0]
    for lo in range(0, x_ref.shape[1], OUTFFN_ROWS):
        rows = slice(lo, lo + OUTFFN_ROWS)
        x = x_ref[0, rows, :]
        f = f_ref[0, rows, :]
        fo = jnp.concatenate(
            [jnp.dot(f[:, g * FG_DIM:(g + 1) * FG_DIM], wf_ref[g], preferred_element_type=F32)
             for g in range(N_FGROUPS)], axis=-1)
        fn = _rms(fo) * gf_ref[...]
        an = _rms(a_ref[0, rows, :].astype(F32)) * ga_ref[...]
        cat = jnp.concatenate([fn, an], axis=-1).astype(BF16)
        mix = jnp.dot(cat, wo_ref[...], preferred_element_type=F32)
        x1 = x + m[2:3] * mix
        h2 = (_rms(x1) * gffn_ref[...] * (1.0 + m[4:5]) + m[3:4]).astype(BF16)
        gate = jnp.dot(h2, wg_ref[...], preferred_element_type=F32)
        up = jnp.dot(h2, wu_ref[...], preferred_element_type=F32)
        act = (gate * jax.nn.sigmoid(gate) * up).astype(BF16)
        ff = jnp.dot(act, wd_ref[...], preferred_element_type=F32)
        o_ref[0, rows, :] = x1 + m[5:6] * ff


def _bf16_const(a):
    return jnp.asarray(a, F32).astype(BF16)


def _const_spec(shape):
    nd = len(shape)
    return pl.BlockSpec(shape, lambda *_: (0,) * nd, pipeline_mode=pl.Buffered(1))


def _adaln(c_all, w_ada, b_ada):
    n = c_all.shape[0]
    tn = 1024
    return pl.pallas_call(
        _adaln_kernel,
        grid=(N_MOD * D_MODEL // tn,),
        in_specs=[pl.BlockSpec((n, D_MODEL), lambda j: (0, 0)),
                  pl.BlockSpec((D_MODEL, tn), lambda j: (0, j)),
                  pl.BlockSpec((1, tn), lambda j: (0, j))],
        out_specs=pl.BlockSpec((n, tn), lambda j: (0, j)),
        out_shape=jax.ShapeDtypeStruct((n, N_MOD * D_MODEL), F32),
        compiler_params=_cparams(1),
        name="adaln",
    )(c_all, w_ada, b_ada.reshape(1, -1))


def _inproj(x, mod, g_attn, w_in, cs, pm, gq, gk, tm):
    b, t, _ = x.shape
    tok = lambda d: pl.BlockSpec((1, tm, d), lambda i, j: (i, j, 0))
    out = jax.ShapeDtypeStruct((b, t, D_FOURIER), BF16)
    kfrac = GRID_W // K_COLS
    colblk = pl.BlockSpec((1, N_COL_BLOCKS, tm // kfrac, D_NA), lambda i, j: (i, 0, j, 0))
    colblk_out = jax.ShapeDtypeStruct((b, N_COL_BLOCKS, t // kfrac, D_NA), BF16)
    return pl.pallas_call(
        _inproj_kernel,
        grid=(b, t // tm),
        in_specs=[tok(D_MODEL),
                  pl.BlockSpec((1, 8, D_MODEL), lambda i, j: (i, 0, 0)),
                  _const_spec((1, D_MODEL)), _const_spec((D_MODEL, D_IN)),
                  _const_spec((FG_DIM, 2 * FG_DIM)), _const_spec((HEAD_MEAN_W, HEAD_MEAN_W)),
                  _const_spec((1, D_NA)), _const_spec((1, D_NA))],
        out_specs=[tok(D_FOURIER)] * 3 + [colblk] * 2,
        out_shape=[out] * 3 + [colblk_out] * 2,
        compiler_params=_cparams(2),
        name="inproj",
    )(x, mod, g_attn, w_in, cs, pm, gq, gk)


def _fourier(ua, ub, n1, n2):
    b, t, d = ua.shape
    assert n1 * n2 == t and n2 % BF16_SUBLANES == 0 and n1 % BF16_SUBLANES == 0
    fa = _bf16_const(_stage_a_matrix(n1))
    gm = _bf16_const(_stage_b_matrices(n1, n2))
    f_block = k1_block = 2 * BF16_SUBLANES
    n_a = n2 // f_block
    n_b = n1 // k1_block
    in_spec = pl.BlockSpec((1, n1, f_block, d), lambda i, j: (i, 0, jnp.minimum(j, n_a - 1), 0))
    out = pl.pallas_call(
        functools.partial(_dft_kernel, n_a_steps=n_a, f_block=f_block, k1_block=k1_block, n2=n2),
        grid=(b, n_a + n_b),
        in_specs=[in_spec, in_spec, _const_spec((2 * n1, 2 * n1)),
                  pl.BlockSpec((k1_block, n2, 2 * n2), lambda i, j: (jnp.maximum(j - n_a, 0), 0, 0))],
        out_specs=pl.BlockSpec((1, n2, k1_block, d), lambda i, j: (i, 0, jnp.maximum(j - n_a, 0), 0)),
        out_shape=jax.ShapeDtypeStruct((b, n2, n1, d), BF16),
        scratch_shapes=[pltpu.VMEM((2 * n1, n2, d), BF16)],
        compiler_params=_cparams(2),
        name="dft",
    )(ua.reshape(b, n1, n2, d), ub.reshape(b, n1, n2, d), fa, gm)
    return out.reshape(b, t, d)


def _natten(q, kc, vc, bias_rows):
    b, t, _ = q.shape
    rows = t // GRID_W
    row_blocks = 8
    tq = row_blocks * Q_ROWS * GRID_W
    tk = kc.shape[2]
    kv_spec = pl.BlockSpec((1, N_COL_BLOCKS, tk, LANES), lambda hb, i, j: (i, 0, 0, hb))
    return pl.pallas_call(
        functools.partial(_natten_kernel, rows=rows, row_blocks=row_blocks),
        grid=(N_HEAD_BLOCKS, b, t // tq),
        in_specs=[pl.BlockSpec((1, tq, LANES), lambda hb, i, j: (i, j, hb)), kv_spec, kv_spec,
                  pl.BlockSpec((1, HEADS_PER_BLOCK, BIAS_ROWS, LANES), lambda hb, i, j: (hb, 0, 0, 0))],
        out_specs=pl.BlockSpec((1, tq, LANES), lambda hb, i, j: (i, j, hb)),
        out_shape=jax.ShapeDtypeStruct((b, t, D_NA), BF16),
        scratch_shapes=[pltpu.VMEM((HEADS_PER_BLOCK, len(ROW_TYPES), N_COL_BLOCKS, UNIT_K, UNIT_Q), F32)],
        compiler_params=_cparams(3),
        name="natten",
    )(q, kc, vc, bias_rows)


def _outffn(x, f, a, mod, gf, ga, wf, wo, gffn, wg, wu, wd, tm):
    b, t, _ = x.shape
    tok = lambda d: pl.BlockSpec((1, tm, d), lambda i, j: (i, j, 0))
    return pl.pallas_call(
        _outffn_kernel,
        grid=(b, t // tm),
        in_specs=[tok(D_MODEL), tok(D_FOURIER), tok(D_NA),
                  pl.BlockSpec((1, 8, D_MODEL), lambda i, j: (i, 0, 0)),
                  _const_spec((1, D_FOURIER)), _const_spec((1, D_NA)),
                  _const_spec((N_FGROUPS, FG_DIM, FG_DIM)), _const_spec((D_MODEL, D_MODEL)),
                  _const_spec((1, D_MODEL)), _const_spec((D_MODEL, D_FF)),
                  _const_spec((D_MODEL, D_FF)), _const_spec((D_FF, D_MODEL))],
        out_specs=tok(D_MODEL),
        out_shape=jax.ShapeDtypeStruct((b, t, D_MODEL), F32),
        compiler_params=_cparams(2),
        name="outffn",
    )(x, f, a, mod, gf, ga, wf, wo, gffn, wg, wu, wd)


def _stage_bias_rows(rpb):
    w = jnp.pad(rpb[:, :, ::-1], ((0, 0), (0, BIAS_ROWS - RPB_ROWS), (0, LANES - RPB_COLS)))
    return w.reshape(N_HEAD_BLOCKS, HEADS_PER_BLOCK, BIAS_ROWS, LANES)


def kernel(x_prompt, x_sample, c_prompt, c_sample, w_ada, b_ada, g_attn, w_in, g_q, g_k, w_fmix, rpb,
           g_fout, g_aout, w_o, g_ffn, w_gate, w_up, w_down):
    assert w_ada.shape[0] == 1
    nb_p, nb_s = c_prompt.shape[0], c_sample.shape[0]
    n_c = nb_p + nb_s
    c_all = jnp.concatenate([c_prompt, c_sample, jnp.zeros((-n_c % 8, D_MODEL), F32)], axis=0)
    mod = _adaln(c_all, w_ada[0], b_ada[0])[:n_c].reshape(n_c, N_MOD, D_MODEL)
    mod = jnp.pad(mod, ((0, 0), (0, 8 - N_MOD), (0, 0)))

    w_in_b = w_in[0].astype(BF16)
    w_o_b = w_o[0].astype(BF16)
    w_g_b = w_gate[0].astype(BF16)
    w_u_b = w_up[0].astype(BF16)
    w_d_b = w_down[0].astype(BF16)
    w_f_b = w_fmix[0].astype(BF16)
    cs = _bf16_const(_channel_dft_matrix())
    pm = _bf16_const(_head_mean_matrix())
    gq = jnp.tile(g_q[0], N_HEADS).reshape(1, D_NA)
    gk = jnp.tile(g_k[0], N_HEADS).reshape(1, D_NA)
    row = lambda v: v.reshape(1, -1)
    bias_rows = _stage_bias_rows(rpb[0])

    def trunk(x, mod_g):
        b, t, _ = x.shape
        ua, ub, q, kc, vc = _inproj(x, mod_g, row(g_attn[0]), w_in_b, cs, pm, gq, gk, tm=1024)
        f = _fourier(ua, ub, t // GRID_W, GRID_W)
        a = _natten(q, kc, vc, bias_rows)
        return _outffn(x, f, a, mod_g, row(g_fout[0]), row(g_aout[0]), w_f_b, w_o_b, row(g_ffn[0]),
                       w_g_b, w_u_b, w_d_b, tm=1024)

    return trunk(x_prompt, mod[:nb_p]), trunk(x_sample, mod[nb_p:])
```

```python
import functools

import numpy as np
import jax
import jax.numpy as jnp
from jax import lax
from jax.experimental import pallas as pl
from jax.experimental.pallas import tpu as pltpu

F32 = jnp.float32
BF16 = jnp.bfloat16

D_MODEL = 1024
GRID_W = 64
D_FOURIER = 512
N_FGROUPS = 4
FG_DIM = 128
D_NA = 512
N_HEADS = 8
HEAD_DIM = 64
WIN_ROWS = 8
WIN_COLS = 16
RPB_ROWS = 2 * WIN_ROWS - 1
RPB_COLS = 2 * WIN_COLS - 1
D_IN = 2048
D_FF = 2816
N_MOD = 6
EPS = 1e-6

LANES = 128
HEAD_MEAN_W = 256
BF16_SUBLANES = 16
HEADS_PER_BLOCK = LANES // HEAD_DIM
N_HEAD_BLOCKS = N_HEADS // HEADS_PER_BLOCK
Q_ROWS = 8
Q_COLS = 16
UNIT_Q = Q_ROWS * Q_COLS
K_ROWS = 16
K_COLS = 32
UNIT_K = K_ROWS * K_COLS
N_COL_BLOCKS = GRID_W // Q_COLS
COL_STARTS = tuple(min(max(Q_COLS * j - WIN_COLS // 2, 0), GRID_W - K_COLS) for j in range(N_COL_BLOCKS))
ROW_TYPES = (0, WIN_ROWS // 2, WIN_ROWS)
BIAS_ROWS = RPB_ROWS + 1
NEG_BIG = -1e30
LOG2E = 1.4426950408889634
OUTFFN_ROWS = 1024

VMEM_LIMIT = 56 * 1024 * 1024


def _cparams(n_axes):
    return pltpu.CompilerParams(
        dimension_semantics=("arbitrary",) * n_axes, vmem_limit_bytes=VMEM_LIMIT)


def _channel_dft_matrix():
    c = np.arange(FG_DIM)
    ang = 2.0 * np.pi * ((c[:, None] * c[None, :]) % FG_DIM) / FG_DIM
    s = 1.0 / np.sqrt(FG_DIM)
    return np.concatenate([np.cos(ang) * s, np.sin(ang) * s], axis=1)


def _stage_a_matrix(n1):
    k = np.arange(n1)
    ang = 2.0 * np.pi * ((k[:, None] * k[None, :]) % n1) / n1
    c, s = np.cos(ang), np.sin(ang)
    re = np.concatenate([c, -s], axis=1)
    im = np.concatenate([-s, -c], axis=1)
    m = np.stack([re, im], axis=1).reshape(2 * n1, 2 * n1)
    return m / np.sqrt(n1)


def _stage_b_matrices(n1, n2):
    t = n1 * n2
    k1 = np.arange(n1)[:, None, None]
    k2 = np.arange(n2)[None, :, None]
    m = np.arange(n2)[None, None, :]
    idx = (m * k2 * n1 + m * k1) % t
    ang = 2.0 * np.pi * idx / t
    g = np.concatenate([np.cos(ang), np.sin(ang)], axis=2)
    return g / np.sqrt(n2)


def _head_mean_matrix():
    h = np.arange(HEAD_MEAN_W) // HEAD_DIM
    return (h[:, None] == h[None, :]).astype(np.float64) / HEAD_DIM


def _window_plan(rows):
    types = {}
    for blk in range(rows // Q_ROWS):
        i0 = Q_ROWS * blk
        ks = min(max(i0 - WIN_ROWS // 2, 0), rows - K_ROWS)
        rel = tuple(min(max(i0 + r - WIN_ROWS // 2, 0), rows - WIN_ROWS) - ks for r in range(Q_ROWS))
        assert all(0 <= x and x + WIN_ROWS <= K_ROWS for x in rel)
        assert types.setdefault(i0 - ks, rel) == rel
    assert sorted(types) == sorted(ROW_TYPES)
    return tuple(types[e] for e in ROW_TYPES)


_WINDOW_PLAN = _window_plan(64)
assert _WINDOW_PLAN == _window_plan(128)
for _j, _cs in enumerate(COL_STARTS):
    for _c in range(Q_COLS * _j, Q_COLS * (_j + 1)):
        _w0 = min(max(_c - WIN_COLS // 2, 0), GRID_W - WIN_COLS)
        assert _cs <= _w0 and _w0 + WIN_COLS <= _cs + K_COLS


def _split_bf16(x):
    hi = x.astype(BF16)
    return hi, (x - hi.astype(F32)).astype(BF16)


def _adaln_kernel(c_ref, w_ref, b_ref, o_ref):
    c = c_ref[...]
    s_hi, s_lo = _split_bf16(c * jax.nn.sigmoid(c))
    w_hi, w_lo = _split_bf16(w_ref[...])
    dot = functools.partial(jnp.dot, preferred_element_type=F32)
    o_ref[...] = dot(s_hi, w_hi) + (dot(s_hi, w_lo) + dot(s_lo, w_hi)) + b_ref[...]


def _rms(x):
    return x * lax.rsqrt(jnp.mean(x * x, axis=-1, keepdims=True) + EPS)


def _head_rms(t, p):
    sq = (t * t).astype(BF16)
    ms = jnp.concatenate(
        [jnp.dot(sq[:, c:c + HEAD_MEAN_W], p, preferred_element_type=F32) for c in range(0, D_NA, HEAD_MEAN_W)],
        axis=-1)
    return t * lax.rsqrt(ms + EPS)


def _inproj_kernel(x_ref, mod_ref, gattn_ref, win_ref, cs_ref, pm_ref, gq_ref, gk_ref,
                   ua_ref, ub_ref, q_ref, kc_ref, va_ref, vb_ref):
    x = x_ref[0]
    m = mod_ref[0]
    h = (_rms(x) * gattn_ref[...] * (1.0 + m[1:2]) + m[0:1]).astype(BF16)
    z = jnp.dot(h, win_ref[...], preferred_element_type=F32)
    cs = cs_ref[...]
    for g in range(N_FGROUPS):
        ug = z[:, g * FG_DIM:(g + 1) * FG_DIM].astype(BF16)
        ab = jnp.dot(ug, cs, preferred_element_type=F32)
        ua_ref[0, :, g * FG_DIM:(g + 1) * FG_DIM] = ab[:, :FG_DIM].astype(BF16)
        ub_ref[0, :, g * FG_DIM:(g + 1) * FG_DIM] = ab[:, FG_DIM:].astype(BF16)
    p = pm_ref[...]
    q = z[:, D_FOURIER:D_FOURIER + D_NA]
    k = z[:, D_FOURIER + D_NA:D_FOURIER + 2 * D_NA]
    q_ref[0] = (_head_rms(q, p) * gq_ref[...] * (HEAD_DIM ** -0.5 * LOG2E)).astype(BF16)
    kn = _head_rms(k, p) * gk_ref[...]
    v = z[:, D_FOURIER + 2 * D_NA:]
    even_head = (lax.broadcasted_iota(jnp.int32, v.shape, 1) & HEAD_DIM) == 0
    for src, dst in ((kn, kc_ref), (jnp.where(even_head, v, 1.0), va_ref), (jnp.where(even_head, 1.0, v), vb_ref)):
        grid = src.reshape(-1, GRID_W, D_NA)
        for j, c0 in enumerate(COL_STARTS):
            dst[0, j] = grid[:, c0:c0 + K_COLS, :].reshape(-1, D_NA).astype(BF16)


def _swap_leading(x):
    return jnp.swapaxes(x, 0, 1)


def _dft_kernel(a_ref, b_ref, f_ref, g_ref, o_ref, y_ref, *, n_a_steps, f_block, k1_block, n2):
    j = pl.program_id(1)

    @pl.when(j < n_a_steps)
    def _():
        at = _swap_leading(a_ref[0])
        bt = _swap_leading(b_ref[0])
        fa = f_ref[...]
        ys = []
        for f in range(f_block):
            x = jnp.concatenate([at[f], bt[f]], axis=0)
            ys.append(jnp.dot(fa, x, preferred_element_type=F32).astype(BF16))
        f0 = pl.multiple_of(j * f_block, f_block)
        y_ref[:, pl.ds(f0, f_block), :] = _swap_leading(jnp.stack(ys, axis=0))

    @pl.when(j >= n_a_steps)
    def _():
        r0 = pl.multiple_of((j - n_a_steps) * (2 * k1_block), 2 * k1_block)
        rs = []
        for i in range(k1_block):
            y = y_ref[pl.ds(r0 + 2 * i, 2)].reshape(2 * n2, D_FOURIER)
            rs.append(jnp.dot(g_ref[i], y, preferred_element_type=F32).astype(BF16))
        o_ref[0] = _swap_leading(jnp.stack(rs, axis=0))


def _build_bias_table(w_ref, tab_ref):
    jc = lax.broadcasted_iota(jnp.int32, (K_COLS, LANES), 0)
    lane = lax.broadcasted_iota(jnp.int32, (K_COLS, LANES), 1)
    c = lane & (Q_COLS - 1)
    neg = jnp.full((K_COLS, LANES), NEG_BIG, F32)
    for j, c0 in enumerate(COL_STARTS):
        wstart = jnp.clip(Q_COLS * j + c - WIN_COLS // 2, 0, GRID_W - WIN_COLS)
        col_ok = (c0 + jc >= wstart) & (c0 + jc < wstart + WIN_COLS)
        o = c0 - Q_COLS * j + WIN_COLS - 1
        in_row = [col_ok & (lane >= Q_COLS * r) & (lane < Q_COLS * (r + 1)) for r in range(Q_ROWS)]
        for h in range(HEADS_PER_BLOCK):
            rolled = {}
            for t, rel in enumerate(_WINDOW_PLAN):
                for jr in range(K_ROWS):
                    slab = neg
                    for r in range(Q_ROWS):
                        if not rel[r] <= jr < rel[r] + WIN_ROWS:
                            continue
                        dr = jr - ROW_TYPES[t] - r + WIN_ROWS - 1
                        assert 0 <= dr < RPB_ROWS
                        if (r, dr) not in rolled:
                            x = jnp.broadcast_to(w_ref[0, h, dr:dr + 1, :], (K_COLS, LANES))
                            base = (Q_COLS * r + o - (RPB_COLS - 1)) % LANES
                            rolled[r, dr] = pltpu.roll(x, base, 1, stride=1, stride_axis=0) * LOG2E
                        slab = jnp.where(in_row[r], rolled[r, dr], slab)
                    tab_ref[h, t, j, pl.ds(jr * K_COLS, K_COLS), :] = slab


def _natten_kernel(q_ref, kc_ref, va_ref, vb_ref, w_ref, o_ref, tab_ref, *, rows, row_blocks):
    @pl.when((pl.program_id(1) == 0) & (pl.program_id(2) == 0))
    def _():
        _build_bias_table(w_ref, tab_ref)

    lane = lax.broadcasted_iota(jnp.int32, (UNIT_Q, LANES), 1)
    first = lane < HEAD_DIM
    vrefs = (va_ref, vb_ref)

    for rb in range(row_blocks):
        i0 = Q_ROWS * (pl.program_id(2) * row_blocks + rb)
        ks = jnp.clip(i0 - WIN_ROWS // 2, 0, rows - K_ROWS)
        rtype = lax.shift_right_logical(i0 - ks, 2)
        kstart = pl.multiple_of(ks * K_COLS, K_COLS)
        tok = slice(rb * Q_ROWS * GRID_W, (rb + 1) * Q_ROWS * GRID_W)
        qb = q_ref[0, tok, :].reshape(Q_ROWS, GRID_W, LANES)
        for j in range(N_COL_BLOCKS):
            cols = slice(j * Q_COLS, (j + 1) * Q_COLS)
            kw = kc_ref[0, j, pl.ds(kstart, UNIT_K), :]
            qp = qb[:, cols, :].reshape(UNIT_Q, LANES).astype(F32)
            outs = []
            for h in range(HEADS_PER_BLOCK):
                qm = jnp.where(first if h == 0 else jnp.logical_not(first), qp, 0.0).astype(BF16)
                s = lax.dot_general(kw, qm, (((1,), (1,)), ((), ())), preferred_element_type=F32)
                s = s + tab_ref[h, rtype, j]
                mx = jnp.max(s, axis=0, keepdims=True)
                pe = jnp.exp2(s - mx).astype(BF16)
                vw = vrefs[h][0, j, pl.ds(kstart, UNIT_K), :]
                outs.append(lax.dot_general(pe, vw, (((0,), (0,)), ((), ())), preferred_element_type=F32))
            num = jnp.where(first, outs[0], outs[1])
            den = pltpu.roll(jnp.where(first, outs[1], outs[0]), HEAD_DIM, 1)
            out = (num * (1.0 / den)).astype(BF16).reshape(Q_ROWS, Q_COLS, LANES)
            for r in range(Q_ROWS):
                o_ref[0, pl.ds(rb * Q_ROWS * GRID_W + r * GRID_W + j * Q_COLS, Q_COLS), :] = out[r]


def _outffn_kernel(x_ref, f_ref, a_ref, mod_ref, gf_ref, ga_ref, wf_ref, wo_ref, gffn_ref,
                   wg_ref, wu_ref, wd_ref, o_ref):
    m = mod_ref[0]
    for lo in range(0, x_ref.shape[1], OUTFFN_ROWS):
        rows = slice(lo, lo + OUTFFN_ROWS)
        x = x_ref[0, rows, :]
        f = f_ref[0, rows, :]
        fo = jnp.concatenate(
            [jnp.dot(f[:, g * FG_DIM:(g + 1) * FG_DIM], wf_ref[g], preferred_element_type=F32)
             for g in range(N_FGROUPS)], axis=-1)
        fn = _rms(fo) * gf_ref[...]
        an = _rms(a_ref[0, rows, :].astype(F32)) * ga_ref[...]
        cat = jnp.concatenate([fn, an], axis=-1).astype(BF16)
        mix = jnp.dot(cat, wo_ref[...], preferred_element_type=F32)
        x1 = x + m[2:3] * mix
        h2 = (_rms(x1) * gffn_ref[...] * (1.0 + m[4:5]) + m[3:4]).astype(BF16)
        gate = jnp.dot(h2, wg_ref[...], preferred_element_type=F32)
        up = jnp.dot(h2, wu_ref[...], preferred_element_type=F32)
        act = (gate * jax.nn.sigmoid(gate) * up).astype(BF16)
        ff = jnp.dot(act, wd_ref[...], preferred_element_type=F32)
        o_ref[0, rows, :] = x1 + m[5:6] * ff


def _bf16_const(a):
    return jnp.asarray(a, F32).astype(BF16)


def _const_spec(shape):
    nd = len(shape)
    return pl.BlockSpec(shape, lambda *_: (0,) * nd, pipeline_mode=pl.Buffered(1))


def _adaln(c_all, w_ada, b_ada):
    n = c_all.shape[0]
    tn = 1024
    return pl.pallas_call(
        _adaln_kernel,
        grid=(N_MOD * D_MODEL // tn,),
        in_specs=[pl.BlockSpec((n, D_MODEL), lambda j: (0, 0)),
                  pl.BlockSpec((D_MODEL, tn), lambda j: (0, j)),
                  pl.BlockSpec((1, tn), lambda j: (0, j))],
        out_specs=pl.BlockSpec((n, tn), lambda j: (0, j)),
        out_shape=jax.ShapeDtypeStruct((n, N_MOD * D_MODEL), F32),
        compiler_params=_cparams(1),
        name="adaln",
    )(c_all, w_ada, b_ada.reshape(1, -1))


def _inproj(x, mod, g_attn, w_in, cs, pm, gq, gk, tm):
    b, t, _ = x.shape
    tok = lambda d: pl.BlockSpec((1, tm, d), lambda i, j: (i, j, 0))
    out = jax.ShapeDtypeStruct((b, t, D_FOURIER), BF16)
    kfrac = GRID_W // K_COLS
    colblk = pl.BlockSpec((1, N_COL_BLOCKS, tm // kfrac, D_NA), lambda i, j: (i, 0, j, 0))
    colblk_out = jax.ShapeDtypeStruct((b, N_COL_BLOCKS, t // kfrac, D_NA), BF16)
    return pl.pallas_call(
        _inproj_kernel,
        grid=(b, t // tm),
        in_specs=[tok(D_MODEL),
                  pl.BlockSpec((1, 8, D_MODEL), lambda i, j: (i, 0, 0)),
                  _const_spec((1, D_MODEL)), _const_spec((D_MODEL, D_IN)),
                  _const_spec((FG_DIM, 2 * FG_DIM)), _const_spec((HEAD_MEAN_W, HEAD_MEAN_W)),
                  _const_spec((1, D_NA)), _const_spec((1, D_NA))],
        out_specs=[tok(D_FOURIER)] * 3 + [colblk] * 3,
        out_shape=[out] * 3 + [colblk_out] * 3,
        compiler_params=_cparams(2),
        name="inproj",
    )(x, mod, g_attn, w_in, cs, pm, gq, gk)


def _fourier(ua, ub, n1, n2):
    b, t, d = ua.shape
    assert n1 * n2 == t and n2 % BF16_SUBLANES == 0 and n1 % BF16_SUBLANES == 0
    fa = _bf16_const(_stage_a_matrix(n1))
    gm = _bf16_const(_stage_b_matrices(n1, n2))
    f_block = k1_block = 2 * BF16_SUBLANES
    n_a = n2 // f_block
    n_b = n1 // k1_block
    in_spec = pl.BlockSpec((1, n1, f_block, d), lambda i, j: (i, 0, jnp.minimum(j, n_a - 1), 0))
    out = pl.pallas_call(
        functools.partial(_dft_kernel, n_a_steps=n_a, f_block=f_block, k1_block=k1_block, n2=n2),
        grid=(b, n_a + n_b),
        in_specs=[in_spec, in_spec, _const_spec((2 * n1, 2 * n1)),
                  pl.BlockSpec((k1_block, n2, 2 * n2), lambda i, j: (jnp.maximum(j - n_a, 0), 0, 0))],
        out_specs=pl.BlockSpec((1, n2, k1_block, d), lambda i, j: (i, 0, jnp.maximum(j - n_a, 0), 0)),
        out_shape=jax.ShapeDtypeStruct((b, n2, n1, d), BF16),
        scratch_shapes=[pltpu.VMEM((2 * n1, n2, d), BF16)],
        compiler_params=_cparams(2),
        name="dft",
    )(ua.reshape(b, n1, n2, d), ub.reshape(b, n1, n2, d), fa, gm)
    return out.reshape(b, t, d)


def _natten(q, kc, va, vb, bias_rows):
    b, t, _ = q.shape
    rows = t // GRID_W
    row_blocks = min(16, rows // Q_ROWS)
    tq = row_blocks * Q_ROWS * GRID_W
    tk = kc.shape[2]
    kv_spec = pl.BlockSpec((1, N_COL_BLOCKS, tk, LANES), lambda hb, i, j: (i, 0, 0, hb))
    return pl.pallas_call(
        functools.partial(_natten_kernel, rows=rows, row_blocks=row_blocks),
        grid=(N_HEAD_BLOCKS, b, t // tq),
        in_specs=[pl.BlockSpec((1, tq, LANES), lambda hb, i, j: (i, j, hb)), kv_spec, kv_spec, kv_spec,
                  pl.BlockSpec((1, HEADS_PER_BLOCK, BIAS_ROWS, LANES), lambda hb, i, j: (hb, 0, 0, 0))],
        out_specs=pl.BlockSpec((1, tq, LANES), lambda hb, i, j: (i, j, hb)),
        out_shape=jax.ShapeDtypeStruct((b, t, D_NA), BF16),
        scratch_shapes=[pltpu.VMEM((HEADS_PER_BLOCK, len(ROW_TYPES), N_COL_BLOCKS, UNIT_K, UNIT_Q), F32)],
        compiler_params=_cparams(3),
        name="natten",
    )(q, kc, va, vb, bias_rows)


def _outffn(x, f, a, mod, gf, ga, wf, wo, gffn, wg, wu, wd, tm):
    b, t, _ = x.shape
    tok = lambda d: pl.BlockSpec((1, tm, d), lambda i, j: (i, j, 0))
    return pl.pallas_call(
        _outffn_kernel,
        grid=(b, t // tm),
        in_specs=[tok(D_MODEL), tok(D_FOURIER), tok(D_NA),
                  pl.BlockSpec((1, 8, D_MODEL), lambda i, j: (i, 0, 0)),
                  _const_spec((1, D_FOURIER)), _const_spec((1, D_NA)),
                  _const_spec((N_FGROUPS, FG_DIM, FG_DIM)), _const_spec((D_MODEL, D_MODEL)),
                  _const_spec((1, D_MODEL)), _const_spec((D_MODEL, D_FF)),
                  _const_spec((D_MODEL, D_FF)), _const_spec((D_FF, D_MODEL))],
        out_specs=tok(D_MODEL),
        out_shape=jax.ShapeDtypeStruct((b, t, D_MODEL), F32),
        compiler_params=_cparams(2),
        name="outffn",
    )(x, f, a, mod, gf, ga, wf, wo, gffn, wg, wu, wd)


def _stage_bias_rows(rpb):
    w = jnp.pad(rpb[:, :, ::-1], ((0, 0), (0, BIAS_ROWS - RPB_ROWS), (0, LANES - RPB_COLS)))
    return w.reshape(N_HEAD_BLOCKS, HEADS_PER_BLOCK, BIAS_ROWS, LANES)


def kernel(x_prompt, x_sample, c_prompt, c_sample, w_ada, b_ada, g_attn, w_in, g_q, g_k, w_fmix, rpb,
           g_fout, g_aout, w_o, g_ffn, w_gate, w_up, w_down):
    assert w_ada.shape[0] == 1
    nb_p, nb_s = c_prompt.shape[0], c_sample.shape[0]
    n_c = nb_p + nb_s
    c_all = jnp.concatenate([c_prompt, c_sample, jnp.zeros((-n_c % 8, D_MODEL), F32)], axis=0)
    mod = _adaln(c_all, w_ada[0], b_ada[0])[:n_c].reshape(n_c, N_MOD, D_MODEL)
    mod = jnp.pad(mod, ((0, 0), (0, 8 - N_MOD), (0, 0)))

    w_in_b = w_in[0].astype(BF16)
    w_o_b = w_o[0].astype(BF16)
    w_g_b = w_gate[0].astype(BF16)
    w_u_b = w_up[0].astype(BF16)
    w_d_b = w_down[0].astype(BF16)
    w_f_b = w_fmix[0].astype(BF16)
    cs = _bf16_const(_channel_dft_matrix())
    pm = _bf16_const(_head_mean_matrix())
    gq = jnp.tile(g_q[0], N_HEADS).reshape(1, D_NA)
    gk = jnp.tile(g_k[0], N_HEADS).reshape(1, D_NA)
    row = lambda v: v.reshape(1, -1)
    bias_rows = _stage_bias_rows(rpb[0])

    def trunk(x, mod_g):
        b, t, _ = x.shape
        ua, ub, q, kc, va, vb = _inproj(x, mod_g, row(g_attn[0]), w_in_b, cs, pm, gq, gk, tm=1024)
        f = _fourier(ua, ub, t // GRID_W, GRID_W)
        a = _natten(q, kc, va, vb, bias_rows)
        return _outffn(x, f, a, mod_g, row(g_fout[0]), row(g_aout[0]), w_f_b, w_o_b, row(g_ffn[0]),
                       w_g_b, w_u_b, w_d_b, tm=1024)

    return trunk(x_prompt, mod[:nb_p]), trunk(x_sample, mod[nb_p:])
```

```python
import functools

import numpy as np
import jax
import jax.numpy as jnp
from jax import lax
from jax.experimental import pallas as pl
from jax.experimental.pallas import tpu as pltpu

F32 = jnp.float32
BF16 = jnp.bfloat16

D_MODEL = 1024
GRID_W = 64
D_FOURIER = 512
N_FGROUPS = 4
FG_DIM = 128
D_NA = 512
N_HEADS = 8
HEAD_DIM = 64
WIN_ROWS = 8
WIN_COLS = 16
RPB_ROWS = 2 * WIN_ROWS - 1
RPB_COLS = 2 * WIN_COLS - 1
D_IN = 2048
D_FF = 2816
N_MOD = 6
EPS = 1e-6

LANES = 128
HEAD_MEAN_W = 256
BF16_SUBLANES = 16
HEADS_PER_BLOCK = LANES // HEAD_DIM
N_HEAD_BLOCKS = N_HEADS // HEADS_PER_BLOCK
Q_ROWS = 8
Q_COLS = 16
UNIT_Q = Q_ROWS * Q_COLS
K_ROWS = 16
K_COLS = 32
UNIT_K = K_ROWS * K_COLS
N_COL_BLOCKS = GRID_W // Q_COLS
COL_STARTS = tuple(min(max(Q_COLS * j - WIN_COLS // 2, 0), GRID_W - K_COLS) for j in range(N_COL_BLOCKS))
ROW_TYPES = (0, WIN_ROWS // 2, WIN_ROWS)
BIAS_ROWS = RPB_ROWS + 1
NEG_BIG = -1e30
LOG2E = 1.4426950408889634

ADALN_TN = 1024
INPROJ_TM = 1024
OUTFFN_TM = 1024
DFT_BLOCK = 2 * BF16_SUBLANES
NATTEN_ROW_BLOCKS = 16
VMEM_LIMIT = 56 * 1024 * 1024


def _cparams(n_axes):
    return pltpu.CompilerParams(
        dimension_semantics=("arbitrary",) * n_axes, vmem_limit_bytes=VMEM_LIMIT)


def _channel_dft_matrix():
    c = np.arange(FG_DIM)
    ang = 2.0 * np.pi * ((c[:, None] * c[None, :]) % FG_DIM) / FG_DIM
    s = 1.0 / np.sqrt(FG_DIM)
    return np.concatenate([np.cos(ang) * s, np.sin(ang) * s], axis=1)


def _stage_a_matrix(n1):
    k = np.arange(n1)
    ang = 2.0 * np.pi * ((k[:, None] * k[None, :]) % n1) / n1
    c, s = np.cos(ang), np.sin(ang)
    re = np.concatenate([c, -s], axis=1)
    im = np.concatenate([-s, -c], axis=1)
    m = np.stack([re, im], axis=1).reshape(2 * n1, 2 * n1)
    return m / np.sqrt(n1)


def _stage_b_matrices(n1, n2):
    t = n1 * n2
    k1 = np.arange(n1)[:, None, None]
    k2 = np.arange(n2)[None, :, None]
    m = np.arange(n2)[None, None, :]
    idx = (m * k2 * n1 + m * k1) % t
    ang = 2.0 * np.pi * idx / t
    g = np.concatenate([np.cos(ang), np.sin(ang)], axis=2)
    return g / np.sqrt(n2)


def _head_mean_matrix():
    h = np.arange(HEAD_MEAN_W) // HEAD_DIM
    return (h[:, None] == h[None, :]).astype(np.float64) / HEAD_DIM


def _window_plan(rows):
    types = {}
    for blk in range(rows // Q_ROWS):
        i0 = Q_ROWS * blk
        ks = min(max(i0 - WIN_ROWS // 2, 0), rows - K_ROWS)
        rel = tuple(min(max(i0 + r - WIN_ROWS // 2, 0), rows - WIN_ROWS) - ks for r in range(Q_ROWS))
        assert all(0 <= x and x + WIN_ROWS <= K_ROWS for x in rel)
        assert types.setdefault(i0 - ks, rel) == rel
    assert sorted(types) == sorted(ROW_TYPES)
    return tuple(types[e] for e in ROW_TYPES)


_WINDOW_PLAN = _window_plan(64)
assert _WINDOW_PLAN == _window_plan(128)
for _j, _cs in enumerate(COL_STARTS):
    for _c in range(Q_COLS * _j, Q_COLS * (_j + 1)):
        _w0 = min(max(_c - WIN_COLS // 2, 0), GRID_W - WIN_COLS)
        assert _cs <= _w0 and _w0 + WIN_COLS <= _cs + K_COLS


def _split_bf16(x):
    hi = x.astype(BF16)
    return hi, (x - hi.astype(F32)).astype(BF16)


def _adaln_kernel(c_ref, w_ref, b_ref, o_ref):
    c = c_ref[...]
    s_hi, s_lo = _split_bf16(c * jax.nn.sigmoid(c))
    w_hi, w_lo = _split_bf16(w_ref[...])
    dot = functools.partial(jnp.dot, preferred_element_type=F32)
    o_ref[...] = dot(s_hi, w_hi) + (dot(s_hi, w_lo) + dot(s_lo, w_hi)) + b_ref[...]


def _rms(x):
    return x * lax.rsqrt(jnp.mean(x * x, axis=-1, keepdims=True) + EPS)


def _head_rms(t, p):
    sq = (t * t).astype(BF16)
    ms = jnp.concatenate(
        [jnp.dot(sq[:, c:c + HEAD_MEAN_W], p, preferred_element_type=F32) for c in range(0, D_NA, HEAD_MEAN_W)],
        axis=-1)
    return t * lax.rsqrt(ms + EPS)


def _inproj_kernel(x_ref, mod_ref, gattn_ref, win_ref, cs_ref, pm_ref, gq_ref, gk_ref,
                   ua_ref, ub_ref, q_ref, kc_ref, va_ref, vb_ref):
    x = x_ref[0]
    m = mod_ref[0]
    h = (_rms(x) * gattn_ref[...] * (1.0 + m[1:2]) + m[0:1]).astype(BF16)
    z = jnp.dot(h, win_ref[...], preferred_element_type=F32)
    cs = cs_ref[...]
    for g in range(N_FGROUPS):
        ug = z[:, g * FG_DIM:(g + 1) * FG_DIM].astype(BF16)
        ab = jnp.dot(ug, cs, preferred_element_type=F32)
        ua_ref[0, :, g * FG_DIM:(g + 1) * FG_DIM] = ab[:, :FG_DIM].astype(BF16)
        ub_ref[0, :, g * FG_DIM:(g + 1) * FG_DIM] = ab[:, FG_DIM:].astype(BF16)
    p = pm_ref[...]
    q = z[:, D_FOURIER:D_FOURIER + D_NA]
    k = z[:, D_FOURIER + D_NA:D_FOURIER + 2 * D_NA]
    q_ref[0] = (_head_rms(q, p) * gq_ref[...] * (HEAD_DIM ** -0.5 * LOG2E)).astype(BF16)
    kn = _head_rms(k, p) * gk_ref[...]
    v = z[:, D_FOURIER + 2 * D_NA:]
    even_head = (lax.broadcasted_iota(jnp.int32, v.shape, 1) & HEAD_DIM) == 0
    for src, dst in ((kn, kc_ref), (jnp.where(even_head, v, 1.0), va_ref), (jnp.where(even_head, 1.0, v), vb_ref)):
        grid = src.reshape(-1, GRID_W, D_NA)
        for j, c0 in enumerate(COL_STARTS):
            dst[0, j] = grid[:, c0:c0 + K_COLS, :].reshape(-1, D_NA).astype(BF16)


def _swap_leading(x):
    return jnp.swapaxes(x, 0, 1)


def _dft_kernel(a_ref, b_ref, f_ref, g_ref, o_ref, y_ref, *, n_a_steps, f_block, k1_block, n2):
    j = pl.program_id(1)

    @pl.when(j < n_a_steps)
    def _():
        at = _swap_leading(a_ref[0])
        bt = _swap_leading(b_ref[0])
        fa = f_ref[...]
        ys = []
        for f in range(f_block):
            x = jnp.concatenate([at[f], bt[f]], axis=0)
            ys.append(jnp.dot(fa, x, preferred_element_type=F32).astype(BF16))
        f0 = pl.multiple_of(j * f_block, f_block)
        y_ref[:, pl.ds(f0, f_block), :] = _swap_leading(jnp.stack(ys, axis=0))

    @pl.when(j >= n_a_steps)
    def _():
        r0 = pl.multiple_of((j - n_a_steps) * (2 * k1_block), 2 * k1_block)
        rs = []
        for i in range(k1_block):
            y = y_ref[pl.ds(r0 + 2 * i, 2)].reshape(2 * n2, D_FOURIER)
            rs.append(jnp.dot(g_ref[i], y, preferred_element_type=F32).astype(BF16))
        o_ref[0] = _swap_leading(jnp.stack(rs, axis=0))


def _build_bias_table(w_ref, tab_ref):
    jc = lax.broadcasted_iota(jnp.int32, (K_COLS, LANES), 0)
    lane = lax.broadcasted_iota(jnp.int32, (K_COLS, LANES), 1)
    c = lane & (Q_COLS - 1)
    neg = jnp.full((K_COLS, LANES), NEG_BIG, F32)
    for j, c0 in enumerate(COL_STARTS):
        wstart = jnp.clip(Q_COLS * j + c - WIN_COLS // 2, 0, GRID_W - WIN_COLS)
        col_ok = (c0 + jc >= wstart) & (c0 + jc < wstart + WIN_COLS)
        o = c0 - Q_COLS * j + WIN_COLS - 1
        in_row = [col_ok & (lane >= Q_COLS * r) & (lane < Q_COLS * (r + 1)) for r in range(Q_ROWS)]
        for h in range(HEADS_PER_BLOCK):
            rolled = {}
            for t, rel in enumerate(_WINDOW_PLAN):
                for jr in range(K_ROWS):
                    slab = neg
                    for r in range(Q_ROWS):
                        if not rel[r] <= jr < rel[r] + WIN_ROWS:
                            continue
                        dr = jr - ROW_TYPES[t] - r + WIN_ROWS - 1
                        assert 0 <= dr < RPB_ROWS
                        if (r, dr) not in rolled:
                            x = jnp.broadcast_to(w_ref[0, h, dr:dr + 1, :] * LOG2E, (K_COLS, LANES))
                            base = (Q_COLS * r + o - (RPB_COLS - 1)) % LANES
                            rolled[r, dr] = pltpu.roll(x, base, 1, stride=1, stride_axis=0)
                        slab = jnp.where(in_row[r], rolled[r, dr], slab)
                    tab_ref[h, t, j, pl.ds(jr * K_COLS, K_COLS), :] = slab


def _natten_kernel(q_ref, kc_ref, va_ref, vb_ref, w_ref, o_ref, tab_ref, *, rows, row_blocks):
    @pl.when((pl.program_id(1) == 0) & (pl.program_id(2) == 0))
    def _():
        _build_bias_table(w_ref, tab_ref)

    lane = lax.broadcasted_iota(jnp.int32, (UNIT_Q, LANES), 1)
    first = lane < HEAD_DIM
    vrefs = (va_ref, vb_ref)

    for rb in range(row_blocks):
        i0 = Q_ROWS * (pl.program_id(2) * row_blocks + rb)
        ks = jnp.clip(i0 - WIN_ROWS // 2, 0, rows - K_ROWS)
        rtype = lax.shift_right_logical(i0 - ks, 2)
        kstart = pl.multiple_of(ks * K_COLS, K_COLS)
        tok = slice(rb * Q_ROWS * GRID_W, (rb + 1) * Q_ROWS * GRID_W)
        qb = q_ref[0, tok, :].reshape(Q_ROWS, GRID_W, LANES)
        for j in range(N_COL_BLOCKS):
            cols = slice(j * Q_COLS, (j + 1) * Q_COLS)
            kw = kc_ref[0, j, pl.ds(kstart, UNIT_K), :]
            qp = qb[:, cols, :].reshape(UNIT_Q, LANES).astype(F32)
            outs = []
            for h in range(HEADS_PER_BLOCK):
                qm = jnp.where(first if h == 0 else jnp.logical_not(first), qp, 0.0).astype(BF16)
                s = lax.dot_general(kw, qm, (((1,), (1,)), ((), ())), preferred_element_type=F32)
                s = s + tab_ref[h, rtype, j]
                mx = jnp.max(s, axis=0, keepdims=True)
                pe = jnp.exp2(s - mx).astype(BF16)
                vw = vrefs[h][0, j, pl.ds(kstart, UNIT_K), :]
                outs.append(lax.dot_general(pe, vw, (((0,), (0,)), ((), ())), preferred_element_type=F32))
            num = jnp.where(first, outs[0], outs[1])
            den = pltpu.roll(jnp.where(first, outs[1], outs[0]), HEAD_DIM, 1)
            out = (num * (1.0 / den)).astype(BF16).reshape(Q_ROWS, Q_COLS, LANES)
            for r in range(Q_ROWS):
                o_ref[0, pl.ds(rb * Q_ROWS * GRID_W + r * GRID_W + j * Q_COLS, Q_COLS), :] = out[r]


def _outffn_kernel(x_ref, f_ref, a_ref, mod_ref, gf_ref, ga_ref, wf_ref, wo_ref, gffn_ref,
                   wg_ref, wu_ref, wd_ref, o_ref):
    x = x_ref[0]
    m = mod_ref[0]
    f = f_ref[0]
    fo = jnp.concatenate(
        [jnp.dot(f[:, g * FG_DIM:(g + 1) * FG_DIM], wf_ref[g], preferred_element_type=F32)
         for g in range(N_FGROUPS)], axis=-1)
    fn = _rms(fo) * gf_ref[...]
    an = _rms(a_ref[0].astype(F32)) * ga_ref[...]
    cat = jnp.concatenate([fn, an], axis=-1).astype(BF16)
    mix = jnp.dot(cat, wo_ref[...], preferred_element_type=F32)
    x1 = x + m[2:3] * mix
    h2 = (_rms(x1) * gffn_ref[...] * (1.0 + m[4:5]) + m[3:4]).astype(BF16)
    gate = jnp.dot(h2, wg_ref[...], preferred_element_type=F32)
    up = jnp.dot(h2, wu_ref[...], preferred_element_type=F32)
    act = (gate * jax.nn.sigmoid(gate) * up).astype(BF16)
    ff = jnp.dot(act, wd_ref[...], preferred_element_type=F32)
    o_ref[0] = x1 + m[5:6] * ff


def _bf16_const(a):
    return jnp.asarray(a, F32).astype(BF16)


def _const_spec(shape):
    nd = len(shape)
    return pl.BlockSpec(shape, lambda *_: (0,) * nd, pipeline_mode=pl.Buffered(1))


def _adaln(c_all, w_ada, b_ada):
    n = c_all.shape[0]
    tn = ADALN_TN
    return pl.pallas_call(
        _adaln_kernel,
        grid=(N_MOD * D_MODEL // tn,),
        in_specs=[pl.BlockSpec((n, D_MODEL), lambda j: (0, 0)),
                  pl.BlockSpec((D_MODEL, tn), lambda j: (0, j)),
                  pl.BlockSpec((1, tn), lambda j: (0, j))],
        out_specs=pl.BlockSpec((n, tn), lambda j: (0, j)),
        out_shape=jax.ShapeDtypeStruct((n, N_MOD * D_MODEL), F32),
        compiler_params=_cparams(1),
        name="adaln",
    )(c_all, w_ada, b_ada.reshape(1, -1))


def _inproj(x, mod, g_attn, w_in, cs, pm, gq, gk, tm):
    b, t, _ = x.shape
    tok = lambda d: pl.BlockSpec((1, tm, d), lambda i, j: (i, j, 0))
    out = jax.ShapeDtypeStruct((b, t, D_FOURIER), BF16)
    kfrac = GRID_W // K_COLS
    colblk = pl.BlockSpec((1, N_COL_BLOCKS, tm // kfrac, D_NA), lambda i, j: (i, 0, j, 0))
    colblk_out = jax.ShapeDtypeStruct((b, N_COL_BLOCKS, t // kfrac, D_NA), BF16)
    return pl.pallas_call(
        _inproj_kernel,
        grid=(b, t // tm),
        in_specs=[tok(D_MODEL),
                  pl.BlockSpec((1, 8, D_MODEL), lambda i, j: (i, 0, 0)),
                  _const_spec((1, D_MODEL)), _const_spec((D_MODEL, D_IN)),
                  _const_spec((FG_DIM, 2 * FG_DIM)), _const_spec((HEAD_MEAN_W, HEAD_MEAN_W)),
                  _const_spec((1, D_NA)), _const_spec((1, D_NA))],
        out_specs=[tok(D_FOURIER)] * 3 + [colblk] * 3,
        out_shape=[out] * 3 + [colblk_out] * 3,
        compiler_params=_cparams(2),
        name="inproj",
    )(x, mod, g_attn, w_in, cs, pm, gq, gk)


def _fourier(ua, ub, n1, n2):
    b, t, d = ua.shape
    assert n1 * n2 == t and n2 % BF16_SUBLANES == 0 and n1 % BF16_SUBLANES == 0
    fa = _bf16_const(_stage_a_matrix(n1))
    gm = _bf16_const(_stage_b_matrices(n1, n2))
    f_block = k1_block = DFT_BLOCK
    n_a = n2 // f_block
    n_b = n1 // k1_block
    in_spec = pl.BlockSpec((1, n1, f_block, d), lambda i, j: (i, 0, jnp.minimum(j, n_a - 1), 0))
    out = pl.pallas_call(
        functools.partial(_dft_kernel, n_a_steps=n_a, f_block=f_block, k1_block=k1_block, n2=n2),
        grid=(b, n_a + n_b),
        in_specs=[in_spec, in_spec, _const_spec((2 * n1, 2 * n1)),
                  pl.BlockSpec((k1_block, n2, 2 * n2), lambda i, j: (jnp.maximum(j - n_a, 0), 0, 0))],
        out_specs=pl.BlockSpec((1, n2, k1_block, d), lambda i, j: (i, 0, jnp.maximum(j - n_a, 0), 0)),
        out_shape=jax.ShapeDtypeStruct((b, n2, n1, d), BF16),
        scratch_shapes=[pltpu.VMEM((2 * n1, n2, d), BF16)],
        compiler_params=_cparams(2),
        name="dft",
    )(ua.reshape(b, n1, n2, d), ub.reshape(b, n1, n2, d), fa, gm)
    return out.reshape(b, t, d)


def _natten(q, kc, va, vb, bias_rows):
    b, t, _ = q.shape
    rows = t // GRID_W
    row_blocks = min(NATTEN_ROW_BLOCKS, rows // Q_ROWS)
    tq = row_blocks * Q_ROWS * GRID_W
    tk = kc.shape[2]
    kv_spec = pl.BlockSpec((1, N_COL_BLOCKS, tk, LANES), lambda hb, i, j: (i, 0, 0, hb))
    return pl.pallas_call(
        functools.partial(_natten_kernel, rows=rows, row_blocks=row_blocks),
        grid=(N_HEAD_BLOCKS, b, t // tq),
        in_specs=[pl.BlockSpec((1, tq, LANES), lambda hb, i, j: (i, j, hb)), kv_spec, kv_spec, kv_spec,
                  pl.BlockSpec((1, HEADS_PER_BLOCK, BIAS_ROWS, LANES), lambda hb, i, j: (hb, 0, 0, 0))],
        out_specs=pl.BlockSpec((1, tq, LANES), lambda hb, i, j: (i, j, hb)),
        out_shape=jax.ShapeDtypeStruct((b, t, D_NA), BF16),
        scratch_shapes=[pltpu.VMEM((HEADS_PER_BLOCK, len(ROW_TYPES), N_COL_BLOCKS, UNIT_K, UNIT_Q), F32)],
        compiler_params=_cparams(3),
        name="natten",
    )(q, kc, va, vb, bias_rows)


def _outffn(x, f, a, mod, gf, ga, wf, wo, gffn, wg, wu, wd, tm):
    b, t, _ = x.shape
    tok = lambda d: pl.BlockSpec((1, tm, d), lambda i, j: (i, j, 0))
    return pl.pallas_call(
        _outffn_kernel,
        grid=(b, t // tm),
        in_specs=[tok(D_MODEL), tok(D_FOURIER), tok(D_NA),
                  pl.BlockSpec((1, 8, D_MODEL), lambda i, j: (i, 0, 0)),
                  _const_spec((1, D_FOURIER)), _const_spec((1, D_NA)),
                  _const_spec((N_FGROUPS, FG_DIM, FG_DIM)), _const_spec((D_MODEL, D_MODEL)),
                  _const_spec((1, D_MODEL)), _const_spec((D_MODEL, D_FF)),
                  _const_spec((D_MODEL, D_FF)), _const_spec((D_FF, D_MODEL))],
        out_specs=tok(D_MODEL),
        out_shape=jax.ShapeDtypeStruct((b, t, D_MODEL), F32),
        compiler_params=_cparams(2),
        name="outffn",
    )(x, f, a, mod, gf, ga, wf, wo, gffn, wg, wu, wd)


def _stage_bias_rows(rpb):
    w = jnp.pad(rpb[:, :, ::-1], ((0, 0), (0, BIAS_ROWS - RPB_ROWS), (0, LANES - RPB_COLS)))
    return w.reshape(N_HEAD_BLOCKS, HEADS_PER_BLOCK, BIAS_ROWS, LANES)


def kernel(x_prompt, x_sample, c_prompt, c_sample, w_ada, b_ada, g_attn, w_in, g_q, g_k, w_fmix, rpb,
           g_fout, g_aout, w_o, g_ffn, w_gate, w_up, w_down):
    assert w_ada.shape[0] == 1
    nb_p, nb_s = c_prompt.shape[0], c_sample.shape[0]
    n_c = nb_p + nb_s
    c_all = jnp.concatenate([c_prompt, c_sample, jnp.zeros((-n_c % 8, D_MODEL), F32)], axis=0)
    mod = _adaln(c_all, w_ada[0], b_ada[0])[:n_c].reshape(n_c, N_MOD, D_MODEL)
    mod = jnp.pad(mod, ((0, 0), (0, 8 - N_MOD), (0, 0)))

    w_in_b = w_in[0].astype(BF16)
    w_o_b = w_o[0].astype(BF16)
    w_g_b = w_gate[0].astype(BF16)
    w_u_b = w_up[0].astype(BF16)
    w_d_b = w_down[0].astype(BF16)
    w_f_b = w_fmix[0].astype(BF16)
    cs = _bf16_const(_channel_dft_matrix())
    pm = _bf16_const(_head_mean_matrix())
    gq = jnp.tile(g_q[0], N_HEADS).reshape(1, D_NA)
    gk = jnp.tile(g_k[0], N_HEADS).reshape(1, D_NA)
    row = lambda v: v.reshape(1, -1)
    bias_rows = _stage_bias_rows(rpb[0])

    def trunk(x, mod_g):
        b, t, _ = x.shape
        ua, ub, q, kc, va, vb = _inproj(x, mod_g, row(g_attn[0]), w_in_b, cs, pm, gq, gk, tm=INPROJ_TM)
        f = _fourier(ua, ub, t // GRID_W, GRID_W)
        a = _natten(q, kc, va, vb, bias_rows)
        return _outffn(x, f, a, mod_g, row(g_fout[0]), row(g_aout[0]), w_f_b, w_o_b, row(g_ffn[0]),
                       w_g_b, w_u_b, w_d_b, tm=OUTFFN_TM)

    return trunk(x_prompt, mod[:nb_p]), trunk(x_sample, mod[nb_p:])
```

```python
import functools

import numpy as np
import jax
import jax.numpy as jnp
from jax import lax
from jax.experimental import pallas as pl
from jax.experimental.pallas import tpu as pltpu

F32 = jnp.float32
BF16 = jnp.bfloat16

D_MODEL = 1024
GRID_W = 64
D_FOURIER = 512
N_FGROUPS = 4
FG_DIM = 128
D_NA = 512
N_HEADS = 8
HEAD_DIM = 64
WIN_ROWS = 8
WIN_COLS = 16
RPB_ROWS = 2 * WIN_ROWS - 1
RPB_COLS = 2 * WIN_COLS - 1
D_IN = 2048
D_FF = 2816
N_MOD = 6
EPS = 1e-6

LANES = 128
HEAD_MEAN_W = 256
BF16_SUBLANES = 16
HEADS_PER_BLOCK = LANES // HEAD_DIM
N_HEAD_BLOCKS = N_HEADS // HEADS_PER_BLOCK
Q_ROWS = 8
Q_COLS = 16
UNIT_Q = Q_ROWS * Q_COLS
K_ROWS = 16
K_COLS = 32
UNIT_K = K_ROWS * K_COLS
N_COL_BLOCKS = GRID_W // Q_COLS
COL_STARTS = tuple(min(max(Q_COLS * j - WIN_COLS // 2, 0), GRID_W - K_COLS) for j in range(N_COL_BLOCKS))
ROW_TYPES = (0, WIN_ROWS // 2, WIN_ROWS)
BIAS_ROWS = RPB_ROWS + 1
NEG_BIG = -1e30
LOG2E = 1.4426950408889634

ADALN_TN = 1024
INPROJ_TM = 1024
OUTFFN_TM = 1024
DFT_BLOCK = 2 * BF16_SUBLANES
NATTEN_ROW_BLOCKS = 16
VMEM_LIMIT = 56 * 1024 * 1024


def _cparams(n_axes):
    return pltpu.CompilerParams(
        dimension_semantics=("arbitrary",) * n_axes, vmem_limit_bytes=VMEM_LIMIT)


def _channel_dft_matrix():
    c = np.arange(FG_DIM)
    ang = 2.0 * np.pi * ((c[:, None] * c[None, :]) % FG_DIM) / FG_DIM
    s = 1.0 / np.sqrt(FG_DIM)
    return np.concatenate([np.cos(ang) * s, np.sin(ang) * s], axis=1)


def _stage_a_matrix(n1):
    k = np.arange(n1)
    ang = 2.0 * np.pi * ((k[:, None] * k[None, :]) % n1) / n1
    c, s = np.cos(ang), np.sin(ang)
    re = np.concatenate([c, -s], axis=1)
    im = np.concatenate([-s, -c], axis=1)
    m = np.stack([re, im], axis=1).reshape(2 * n1, 2 * n1)
    return m / np.sqrt(n1)


def _stage_b_matrices(n1, n2):
    t = n1 * n2
    k1 = np.arange(n1)[:, None, None]
    k2 = np.arange(n2)[None, :, None]
    m = np.arange(n2)[None, None, :]
    idx = (m * k2 * n1 + m * k1) % t
    ang = 2.0 * np.pi * idx / t
    g = np.concatenate([np.cos(ang), np.sin(ang)], axis=2)
    return g / np.sqrt(n2)


def _head_mean_matrix():
    h = np.arange(HEAD_MEAN_W) // HEAD_DIM
    return (h[:, None] == h[None, :]).astype(np.float64) / HEAD_DIM


def _window_plan(rows):
    types = {}
    for blk in range(rows // Q_ROWS):
        i0 = Q_ROWS * blk
        ks = min(max(i0 - WIN_ROWS // 2, 0), rows - K_ROWS)
        rel = tuple(min(max(i0 + r - WIN_ROWS // 2, 0), rows - WIN_ROWS) - ks for r in range(Q_ROWS))
        assert all(0 <= x and x + WIN_ROWS <= K_ROWS for x in rel)
        assert types.setdefault(i0 - ks, rel) == rel
    assert sorted(types) == sorted(ROW_TYPES)
    return tuple(types[e] for e in ROW_TYPES)


_WINDOW_PLAN = _window_plan(64)
assert _WINDOW_PLAN == _window_plan(128)
for _j, _cs in enumerate(COL_STARTS):
    for _c in range(Q_COLS * _j, Q_COLS * (_j + 1)):
        _w0 = min(max(_c - WIN_COLS // 2, 0), GRID_W - WIN_COLS)
        assert _cs <= _w0 and _w0 + WIN_COLS <= _cs + K_COLS


def _split_bf16(x):
    hi = x.astype(BF16)
    return hi, (x - hi.astype(F32)).astype(BF16)


def _adaln_kernel(c_ref, w_ref, b_ref, o_ref):
    c = c_ref[...]
    s_hi, s_lo = _split_bf16(c * jax.nn.sigmoid(c))
    w_hi, w_lo = _split_bf16(w_ref[...])
    dot = functools.partial(jnp.dot, preferred_element_type=F32)
    o_ref[...] = dot(s_hi, w_hi) + (dot(s_hi, w_lo) + dot(s_lo, w_hi)) + b_ref[...]


def _rms(x):
    return x * lax.rsqrt(jnp.mean(x * x, axis=-1, keepdims=True) + EPS)


def _head_rms(t, p):
    sq = (t * t).astype(BF16)
    ms = jnp.concatenate(
        [jnp.dot(sq[:, c:c + HEAD_MEAN_W], p, preferred_element_type=F32) for c in range(0, D_NA, HEAD_MEAN_W)],
        axis=-1)
    return t * lax.rsqrt(ms + EPS)


def _inproj_kernel(x_ref, mod_ref, gattn_ref, win_ref, pm_ref, gq_ref, gk_ref,
                   u_ref, q_ref, kc_ref, va_ref, vb_ref):
    x = x_ref[0]
    m = mod_ref[0]
    h = (_rms(x) * gattn_ref[...] * (1.0 + m[1:2]) + m[0:1]).astype(BF16)
    z = jnp.dot(h, win_ref[...], preferred_element_type=F32)
    u_ref[0] = z[:, :D_FOURIER].astype(BF16)
    p = pm_ref[...]
    q = z[:, D_FOURIER:D_FOURIER + D_NA]
    k = z[:, D_FOURIER + D_NA:D_FOURIER + 2 * D_NA]
    q_ref[0] = (_head_rms(q, p) * gq_ref[...] * (HEAD_DIM ** -0.5 * LOG2E)).astype(BF16)
    kn = _head_rms(k, p) * gk_ref[...]
    v = z[:, D_FOURIER + 2 * D_NA:]
    even_head = (lax.broadcasted_iota(jnp.int32, v.shape, 1) & HEAD_DIM) == 0
    for src, dst in ((kn, kc_ref), (jnp.where(even_head, v, 1.0), va_ref), (jnp.where(even_head, 1.0, v), vb_ref)):
        grid = src.reshape(-1, GRID_W, D_NA)
        for j, c0 in enumerate(COL_STARTS):
            dst[0, j] = grid[:, c0:c0 + K_COLS, :].reshape(-1, D_NA).astype(BF16)


def _swap_leading(x):
    return jnp.swapaxes(x, 0, 1)


def _dft_kernel(u_ref, cs_ref, f_ref, g_ref, o_ref, y_ref, *, n_a_steps, f_block, k1_block, n2):
    j = pl.program_id(1)

    @pl.when(j < n_a_steps)
    def _():
        ut = _swap_leading(u_ref[0])
        n1 = ut.shape[1]
        fa = f_ref[...]
        cs = cs_ref[...]
        ab = [jnp.dot(ut[:, :, g * FG_DIM:(g + 1) * FG_DIM].reshape(f_block * n1, FG_DIM), cs,
                      preferred_element_type=F32).astype(BF16).reshape(f_block, n1, 2 * FG_DIM)
              for g in range(N_FGROUPS)]
        ys = []
        for f in range(f_block):
            x = jnp.concatenate([jnp.concatenate([ab[g][f, :, :FG_DIM] for g in range(N_FGROUPS)], axis=-1),
                                 jnp.concatenate([ab[g][f, :, FG_DIM:] for g in range(N_FGROUPS)], axis=-1)],
                                axis=0)
            ys.append(jnp.dot(fa, x, preferred_element_type=F32).astype(BF16))
        f0 = pl.multiple_of(j * f_block, f_block)
        y_ref[:, pl.ds(f0, f_block), :] = _swap_leading(jnp.stack(ys, axis=0))

    @pl.when(j >= n_a_steps)
    def _():
        r0 = pl.multiple_of((j - n_a_steps) * (2 * k1_block), 2 * k1_block)
        rs = []
        for i in range(k1_block):
            y = y_ref[pl.ds(r0 + 2 * i, 2)].reshape(2 * n2, D_FOURIER)
            rs.append(jnp.dot(g_ref[i], y, preferred_element_type=F32).astype(BF16))
        o_ref[0] = _swap_leading(jnp.stack(rs, axis=0))


def _build_bias_table(w_ref, tab_ref):
    jc = lax.broadcasted_iota(jnp.int32, (K_COLS, LANES), 0)
    lane = lax.broadcasted_iota(jnp.int32, (K_COLS, LANES), 1)
    c = lane & (Q_COLS - 1)
    neg = jnp.full((K_COLS, LANES), NEG_BIG, F32)
    for j, c0 in enumerate(COL_STARTS):
        wstart = jnp.clip(Q_COLS * j + c - WIN_COLS // 2, 0, GRID_W - WIN_COLS)
        col_ok = (c0 + jc >= wstart) & (c0 + jc < wstart + WIN_COLS)
        o = c0 - Q_COLS * j + WIN_COLS - 1
        in_row = [col_ok & (lane >= Q_COLS * r) & (lane < Q_COLS * (r + 1)) for r in range(Q_ROWS)]
        for h in range(HEADS_PER_BLOCK):
            rolled = {}
            for t, rel in enumerate(_WINDOW_PLAN):
                for jr in range(K_ROWS):
                    slab = neg
                    for r in range(Q_ROWS):
                        if not rel[r] <= jr < rel[r] + WIN_ROWS:
                            continue
                        dr = jr - ROW_TYPES[t] - r + WIN_ROWS - 1
                        assert 0 <= dr < RPB_ROWS
                        if (r, dr) not in rolled:
                            x = jnp.broadcast_to(w_ref[0, h, dr:dr + 1, :] * LOG2E, (K_COLS, LANES))
                            base = (Q_COLS * r + o - (RPB_COLS - 1)) % LANES
                            rolled[r, dr] = pltpu.roll(x, base, 1, stride=1, stride_axis=0)
                        slab = jnp.where(in_row[r], rolled[r, dr], slab)
                    tab_ref[h, t, j, pl.ds(jr * K_COLS, K_COLS), :] = slab


def _natten_kernel(q_ref, kc_ref, va_ref, vb_ref, w_ref, o_ref, tab_ref, *, rows, row_blocks):
    @pl.when((pl.program_id(1) == 0) & (pl.program_id(2) == 0))
    def _():
        _build_bias_table(w_ref, tab_ref)

    lane = lax.broadcasted_iota(jnp.int32, (UNIT_Q, LANES), 1)
    first = lane < HEAD_DIM
    vrefs = (va_ref, vb_ref)

    for rb in range(row_blocks):
        i0 = Q_ROWS * (pl.program_id(2) * row_blocks + rb)
        ks = jnp.clip(i0 - WIN_ROWS // 2, 0, rows - K_ROWS)
        rtype = lax.shift_right_logical(i0 - ks, 2)
        kstart = pl.multiple_of(ks * K_COLS, K_COLS)
        tok = slice(rb * Q_ROWS * GRID_W, (rb + 1) * Q_ROWS * GRID_W)
        qb = q_ref[0, tok, :].reshape(Q_ROWS, GRID_W, LANES)
        for j in range(N_COL_BLOCKS):
            cols = slice(j * Q_COLS, (j + 1) * Q_COLS)
            kw = kc_ref[0, j, pl.ds(kstart, UNIT_K), :]
            qp = qb[:, cols, :].reshape(UNIT_Q, LANES).astype(F32)
            outs = []
            for h in range(HEADS_PER_BLOCK):
                qm = jnp.where(first if h == 0 else jnp.logical_not(first), qp, 0.0).astype(BF16)
                s = lax.dot_general(kw, qm, (((1,), (1,)), ((), ())), preferred_element_type=F32)
                s = s + tab_ref[h, rtype, j]
                mx = jnp.max(s, axis=0, keepdims=True)
                pe = jnp.exp2(s - mx).astype(BF16)
                vw = vrefs[h][0, j, pl.ds(kstart, UNIT_K), :]
                outs.append(lax.dot_general(pe, vw, (((0,), (0,)), ((), ())), preferred_element_type=F32))
            num = jnp.where(first, outs[0], outs[1])
            den = pltpu.roll(jnp.where(first, outs[1], outs[0]), HEAD_DIM, 1)
            out = (num * (1.0 / den)).astype(BF16).reshape(Q_ROWS, Q_COLS, LANES)
            for r in range(Q_ROWS):
                o_ref[0, pl.ds(rb * Q_ROWS * GRID_W + r * GRID_W + j * Q_COLS, Q_COLS), :] = out[r]


def _outffn_kernel(x_ref, f_ref, a_ref, mod_ref, gf_ref, ga_ref, wf_ref, wo_ref, gffn_ref,
                   wg_ref, wu_ref, wd_ref, o_ref):
    x = x_ref[0]
    m = mod_ref[0]
    f = f_ref[0]
    fo = jnp.concatenate(
        [jnp.dot(f[:, g * FG_DIM:(g + 1) * FG_DIM], wf_ref[g], preferred_element_type=F32)
         for g in range(N_FGROUPS)], axis=-1)
    fn = _rms(fo) * gf_ref[...]
    an = _rms(a_ref[0].astype(F32)) * ga_ref[...]
    cat = jnp.concatenate([fn, an], axis=-1).astype(BF16)
    mix = jnp.dot(cat, wo_ref[...], preferred_element_type=F32)
    x1 = x + m[2:3] * mix
    h2 = (_rms(x1) * gffn_ref[...] * (1.0 + m[4:5]) + m[3:4]).astype(BF16)
    gate = jnp.dot(h2, wg_ref[...], preferred_element_type=F32)
    up = jnp.dot(h2, wu_ref[...], preferred_element_type=F32)
    act = (gate * jax.nn.sigmoid(gate) * up).astype(BF16)
    ff = jnp.dot(act, wd_ref[...], preferred_element_type=F32)
    o_ref[0] = x1 + m[5:6] * ff


def _bf16_const(a):
    return jnp.asarray(a, F32).astype(BF16)


def _const_spec(shape):
    nd = len(shape)
    return pl.BlockSpec(shape, lambda *_: (0,) * nd, pipeline_mode=pl.Buffered(1))


def _adaln(c_all, w_ada, b_ada):
    n = c_all.shape[0]
    tn = ADALN_TN
    return pl.pallas_call(
        _adaln_kernel,
        grid=(N_MOD * D_MODEL // tn,),
        in_specs=[pl.BlockSpec((n, D_MODEL), lambda j: (0, 0)),
                  pl.BlockSpec((D_MODEL, tn), lambda j: (0, j)),
                  pl.BlockSpec((1, tn), lambda j: (0, j))],
        out_specs=pl.BlockSpec((n, tn), lambda j: (0, j)),
        out_shape=jax.ShapeDtypeStruct((n, N_MOD * D_MODEL), F32),
        compiler_params=_cparams(1),
        name="adaln",
    )(c_all, w_ada, b_ada.reshape(1, -1))


def _inproj(x, mod, g_attn, w_in, pm, gq, gk, tm):
    b, t, _ = x.shape
    tok = lambda d: pl.BlockSpec((1, tm, d), lambda i, j: (i, j, 0))
    out = jax.ShapeDtypeStruct((b, t, D_FOURIER), BF16)
    kfrac = GRID_W // K_COLS
    colblk = pl.BlockSpec((1, N_COL_BLOCKS, tm // kfrac, D_NA), lambda i, j: (i, 0, j, 0))
    colblk_out = jax.ShapeDtypeStruct((b, N_COL_BLOCKS, t // kfrac, D_NA), BF16)
    return pl.pallas_call(
        _inproj_kernel,
        grid=(b, t // tm),
        in_specs=[tok(D_MODEL),
                  pl.BlockSpec((1, 8, D_MODEL), lambda i, j: (i, 0, 0)),
                  _const_spec((1, D_MODEL)), _const_spec((D_MODEL, D_IN)),
                  _const_spec((HEAD_MEAN_W, HEAD_MEAN_W)),
                  _const_spec((1, D_NA)), _const_spec((1, D_NA))],
        out_specs=[tok(D_FOURIER)] * 2 + [colblk] * 3,
        out_shape=[out] * 2 + [colblk_out] * 3,
        compiler_params=_cparams(2),
        name="inproj",
    )(x, mod, g_attn, w_in, pm, gq, gk)


def _fourier(u, cs, n1, n2):
    b, t, d = u.shape
    assert n1 * n2 == t and n2 % BF16_SUBLANES == 0 and n1 % BF16_SUBLANES == 0
    fa = _bf16_const(_stage_a_matrix(n1))
    gm = _bf16_const(_stage_b_matrices(n1, n2))
    f_block = k1_block = DFT_BLOCK
    n_a = n2 // f_block
    n_b = n1 // k1_block
    in_spec = pl.BlockSpec((1, n1, f_block, d), lambda i, j: (i, 0, jnp.minimum(j, n_a - 1), 0))
    out = pl.pallas_call(
        functools.partial(_dft_kernel, n_a_steps=n_a, f_block=f_block, k1_block=k1_block, n2=n2),
        grid=(b, n_a + n_b),
        in_specs=[in_spec, _const_spec((FG_DIM, 2 * FG_DIM)), _const_spec((2 * n1, 2 * n1)),
                  pl.BlockSpec((k1_block, n2, 2 * n2), lambda i, j: (jnp.maximum(j - n_a, 0), 0, 0))],
        out_specs=pl.BlockSpec((1, n2, k1_block, d), lambda i, j: (i, 0, jnp.maximum(j - n_a, 0), 0)),
        out_shape=jax.ShapeDtypeStruct((b, n2, n1, d), BF16),
        scratch_shapes=[pltpu.VMEM((2 * n1, n2, d), BF16)],
        compiler_params=_cparams(2),
        name="dft",
    )(u.reshape(b, n1, n2, d), cs, fa, gm)
    return out.reshape(b, t, d)


def _natten(q, kc, va, vb, bias_rows):
    b, t, _ = q.shape
    rows = t // GRID_W
    row_blocks = min(NATTEN_ROW_BLOCKS, rows // Q_ROWS)
    tq = row_blocks * Q_ROWS * GRID_W
    tk = kc.shape[2]
    kv_spec = pl.BlockSpec((1, N_COL_BLOCKS, tk, LANES), lambda hb, i, j: (i, 0, 0, hb))
    return pl.pallas_call(
        functools.partial(_natten_kernel, rows=rows, row_blocks=row_blocks),
        grid=(N_HEAD_BLOCKS, b, t // tq),
        in_specs=[pl.BlockSpec((1, tq, LANES), lambda hb, i, j: (i, j, hb)), kv_spec, kv_spec, kv_spec,
                  pl.BlockSpec((1, HEADS_PER_BLOCK, BIAS_ROWS, LANES), lambda hb, i, j: (hb, 0, 0, 0))],
        out_specs=pl.BlockSpec((1, tq, LANES), lambda hb, i, j: (i, j, hb)),
        out_shape=jax.ShapeDtypeStruct((b, t, D_NA), BF16),
        scratch_shapes=[pltpu.VMEM((HEADS_PER_BLOCK, len(ROW_TYPES), N_COL_BLOCKS, UNIT_K, UNIT_Q), F32)],
        compiler_params=_cparams(3),
        name="natten",
    )(q, kc, va, vb, bias_rows)


def _outffn(x, f, a, mod, gf, ga, wf, wo, gffn, wg, wu, wd, tm):
    b, t, _ = x.shape
    tok = lambda d: pl.BlockSpec((1, tm, d), lambda i, j: (i, j, 0))
    return pl.pallas_call(
        _outffn_kernel,
        grid=(b, t // tm),
        in_specs=[tok(D_MODEL), tok(D_FOURIER), tok(D_NA),
                  pl.BlockSpec((1, 8, D_MODEL), lambda i, j: (i, 0, 0)),
                  _const_spec((1, D_FOURIER)), _const_spec((1, D_NA)),
                  _const_spec((N_FGROUPS, FG_DIM, FG_DIM)), _const_spec((D_MODEL, D_MODEL)),
                  _const_spec((1, D_MODEL)), _const_spec((D_MODEL, D_FF)),
                  _const_spec((D_MODEL, D_FF)), _const_spec((D_FF, D_MODEL))],
        out_specs=tok(D_MODEL),
        out_shape=jax.ShapeDtypeStruct((b, t, D_MODEL), F32),
        compiler_params=_cparams(2),
        name="outffn",
    )(x, f, a, mod, gf, ga, wf, wo, gffn, wg, wu, wd)


def _stage_bias_rows(rpb):
    w = jnp.pad(rpb[:, :, ::-1], ((0, 0), (0, BIAS_ROWS - RPB_ROWS), (0, LANES - RPB_COLS)))
    return w.reshape(N_HEAD_BLOCKS, HEADS_PER_BLOCK, BIAS_ROWS, LANES)


def kernel(x_prompt, x_sample, c_prompt, c_sample, w_ada, b_ada, g_attn, w_in, g_q, g_k, w_fmix, rpb,
           g_fout, g_aout, w_o, g_ffn, w_gate, w_up, w_down):
    assert w_ada.shape[0] == 1
    nb_p, nb_s = c_prompt.shape[0], c_sample.shape[0]
    n_c = nb_p + nb_s
    c_all = jnp.concatenate([c_prompt, c_sample, jnp.zeros((-n_c % 8, D_MODEL), F32)], axis=0)
    mod = _adaln(c_all, w_ada[0], b_ada[0])[:n_c].reshape(n_c, N_MOD, D_MODEL)
    mod = jnp.pad(mod, ((0, 0), (0, 8 - N_MOD), (0, 0)))

    w_in_b = w_in[0].astype(BF16)
    w_o_b = w_o[0].astype(BF16)
    w_g_b = w_gate[0].astype(BF16)
    w_u_b = w_up[0].astype(BF16)
    w_d_b = w_down[0].astype(BF16)
    w_f_b = w_fmix[0].astype(BF16)
    cs = _bf16_const(_channel_dft_matrix())
    pm = _bf16_const(_head_mean_matrix())
    gq = jnp.tile(g_q[0], N_HEADS).reshape(1, D_NA)
    gk = jnp.tile(g_k[0], N_HEADS).reshape(1, D_NA)
    row = lambda v: v.reshape(1, -1)
    bias_rows = _stage_bias_rows(rpb[0])

    def trunk(x, mod_g):
        b, t, _ = x.shape
        u, q, kc, va, vb = _inproj(x, mod_g, row(g_attn[0]), w_in_b, pm, gq, gk, tm=INPROJ_TM)
        f = _fourier(u, cs, t // GRID_W, GRID_W)
        a = _natten(q, kc, va, vb, bias_rows)
        return _outffn(x, f, a, mod_g, row(g_fout[0]), row(g_aout[0]), w_f_b, w_o_b, row(g_ffn[0]),
                       w_g_b, w_u_b, w_d_b, tm=OUTFFN_TM)

    return trunk(x_prompt, mod[:nb_p]), trunk(x_sample, mod[nb_p:])
```

```python
import functools

import numpy as np
import jax
import jax.numpy as jnp
from jax import lax
from jax.experimental import pallas as pl
from jax.experimental.pallas import tpu as pltpu

F32 = jnp.float32
BF16 = jnp.bfloat16

D_MODEL = 1024
GRID_W = 64
D_FOURIER = 512
N_FGROUPS = 4
FG_DIM = 128
D_NA = 512
N_HEADS = 8
HEAD_DIM = 64
WIN_ROWS = 8
WIN_COLS = 16
RPB_ROWS = 2 * WIN_ROWS - 1
RPB_COLS = 2 * WIN_COLS - 1
D_IN = 2048
D_FF = 2816
N_MOD = 6
EPS = 1e-6

LANES = 128
HEAD_MEAN_W = 256
BF16_SUBLANES = 16
HEADS_PER_BLOCK = LANES // HEAD_DIM
N_HEAD_BLOCKS = N_HEADS // HEADS_PER_BLOCK
Q_ROWS = 8
Q_COLS = 16
UNIT_Q = Q_ROWS * Q_COLS
K_ROWS = 16
K_COLS = 32
UNIT_K = K_ROWS * K_COLS
N_COL_BLOCKS = GRID_W // Q_COLS
COL_STARTS = tuple(min(max(Q_COLS * j - WIN_COLS // 2, 0), GRID_W - K_COLS) for j in range(N_COL_BLOCKS))
ROW_TYPES = (0, WIN_ROWS // 2, WIN_ROWS)
BIAS_ROWS = RPB_ROWS + 1
NEG_BIG = -1e30
LOG2E = 1.4426950408889634

ADALN_TN = 1024
INPROJ_TM = 1024
OUTFFN_TM = 1024
DFT_BLOCK = 2 * BF16_SUBLANES
NATTEN_ROW_BLOCKS = 16
VMEM_LIMIT = 56 * 1024 * 1024


def _cparams(n_axes):
    return pltpu.CompilerParams(
        dimension_semantics=("arbitrary",) * n_axes, vmem_limit_bytes=VMEM_LIMIT)


def _channel_dft_matrix():
    c = np.arange(FG_DIM)
    ang = 2.0 * np.pi * ((c[:, None] * c[None, :]) % FG_DIM) / FG_DIM
    s = 1.0 / np.sqrt(FG_DIM)
    return np.concatenate([np.cos(ang) * s, np.sin(ang) * s], axis=1)


def _stage_a_matrix(n1):
    k = np.arange(n1)
    ang = 2.0 * np.pi * ((k[:, None] * k[None, :]) % n1) / n1
    c, s = np.cos(ang), np.sin(ang)
    re = np.concatenate([c, -s], axis=1)
    im = np.concatenate([-s, -c], axis=1)
    m = np.stack([re, im], axis=1).reshape(2 * n1, 2 * n1)
    return m / np.sqrt(n1)


def _stage_b_matrices(n1, n2):
    t = n1 * n2
    k1 = np.arange(n1)[:, None, None]
    k2 = np.arange(n2)[None, :, None]
    m = np.arange(n2)[None, None, :]
    idx = (m * k2 * n1 + m * k1) % t
    ang = 2.0 * np.pi * idx / t
    g = np.concatenate([np.cos(ang), np.sin(ang)], axis=2)
    return g / np.sqrt(n2)


def _head_mean_matrix():
    h = np.arange(HEAD_MEAN_W) // HEAD_DIM
    return (h[:, None] == h[None, :]).astype(np.float64) / HEAD_DIM


def _window_plan(rows):
    types = {}
    for blk in range(rows // Q_ROWS):
        i0 = Q_ROWS * blk
        ks = min(max(i0 - WIN_ROWS // 2, 0), rows - K_ROWS)
        rel = tuple(min(max(i0 + r - WIN_ROWS // 2, 0), rows - WIN_ROWS) - ks for r in range(Q_ROWS))
        assert all(0 <= x and x + WIN_ROWS <= K_ROWS for x in rel)
        assert types.setdefault(i0 - ks, rel) == rel
    assert sorted(types) == sorted(ROW_TYPES)
    return tuple(types[e] for e in ROW_TYPES)


_WINDOW_PLAN = _window_plan(64)
assert _WINDOW_PLAN == _window_plan(128)
for _j, _cs in enumerate(COL_STARTS):
    for _c in range(Q_COLS * _j, Q_COLS * (_j + 1)):
        _w0 = min(max(_c - WIN_COLS // 2, 0), GRID_W - WIN_COLS)
        assert _cs <= _w0 and _w0 + WIN_COLS <= _cs + K_COLS


def _split_bf16(x):
    hi = x.astype(BF16)
    return hi, (x - hi.astype(F32)).astype(BF16)


def _adaln_kernel(c_ref, w_ref, b_ref, o_ref):
    c = c_ref[...]
    s_hi, s_lo = _split_bf16(c * jax.nn.sigmoid(c))
    w_hi, w_lo = _split_bf16(w_ref[...])
    dot = functools.partial(jnp.dot, preferred_element_type=F32)
    o_ref[...] = dot(s_hi, w_hi) + (dot(s_hi, w_lo) + dot(s_lo, w_hi)) + b_ref[...]


def _rms(x):
    return x * lax.rsqrt(jnp.mean(x * x, axis=-1, keepdims=True) + EPS)


def _head_rms(t, p):
    sq = (t * t).astype(BF16)
    ms = jnp.concatenate(
        [jnp.dot(sq[:, c:c + HEAD_MEAN_W], p, preferred_element_type=F32) for c in range(0, D_NA, HEAD_MEAN_W)],
        axis=-1)
    return t * lax.rsqrt(ms + EPS)


def _inproj_kernel(x_ref, mod_ref, gattn_ref, win_ref, pm_ref, gq_ref, gk_ref,
                   u_ref, q_ref, kc_ref, va_ref, vb_ref):
    x = x_ref[0]
    m = mod_ref[0]
    h = (_rms(x) * gattn_ref[...] * (1.0 + m[1:2]) + m[0:1]).astype(BF16)
    z = jnp.dot(h, win_ref[...], preferred_element_type=F32)
    u_ref[0] = z[:, :D_FOURIER].astype(BF16)
    p = pm_ref[...]
    q = z[:, D_FOURIER:D_FOURIER + D_NA]
    k = z[:, D_FOURIER + D_NA:D_FOURIER + 2 * D_NA]
    q_ref[0] = (_head_rms(q, p) * gq_ref[...] * (HEAD_DIM ** -0.5 * LOG2E)).astype(BF16)
    kn = _head_rms(k, p) * gk_ref[...]
    v = z[:, D_FOURIER + 2 * D_NA:]
    even_head = (lax.broadcasted_iota(jnp.int32, v.shape, 1) & HEAD_DIM) == 0
    for src, dst in ((kn, kc_ref), (jnp.where(even_head, v, 1.0), va_ref), (jnp.where(even_head, 1.0, v), vb_ref)):
        grid = src.reshape(-1, GRID_W, D_NA)
        for j, c0 in enumerate(COL_STARTS):
            dst[0, j] = grid[:, c0:c0 + K_COLS, :].reshape(-1, D_NA).astype(BF16)


def _swap_leading(x):
    return jnp.swapaxes(x, 0, 1)


def _dft_kernel(u_ref, cs_ref, f_ref, g_ref, o_ref, y_ref, *, n_a_steps, f_block, k1_block, n2):
    j = pl.program_id(1)

    @pl.when(j < n_a_steps)
    def _():
        ut = _swap_leading(u_ref[0])
        n1 = ut.shape[1]
        fa = f_ref[...]
        cs = cs_ref[...]
        ab = [jnp.dot(ut[:, :, g * FG_DIM:(g + 1) * FG_DIM].reshape(f_block * n1, FG_DIM), cs,
                      preferred_element_type=F32).astype(BF16).reshape(f_block, n1, 2 * FG_DIM)
              for g in range(N_FGROUPS)]
        ys = []
        for f in range(f_block):
            x = jnp.concatenate([jnp.concatenate([ab[g][f, :, :FG_DIM] for g in range(N_FGROUPS)], axis=-1),
                                 jnp.concatenate([ab[g][f, :, FG_DIM:] for g in range(N_FGROUPS)], axis=-1)],
                                axis=0)
            ys.append(jnp.dot(fa, x, preferred_element_type=F32).astype(BF16))
        f0 = pl.multiple_of(j * f_block, f_block)
        y_ref[:, pl.ds(f0, f_block), :] = _swap_leading(jnp.stack(ys, axis=0))

    @pl.when(j >= n_a_steps)
    def _():
        r0 = pl.multiple_of((j - n_a_steps) * (2 * k1_block), 2 * k1_block)
        rs = []
        for i in range(k1_block):
            y = y_ref[pl.ds(r0 + 2 * i, 2)].reshape(2 * n2, D_FOURIER)
            rs.append(jnp.dot(g_ref[i], y, preferred_element_type=F32).astype(BF16))
        o_ref[0] = _swap_leading(jnp.stack(rs, axis=0))


def _build_bias_table(w_ref, tab_ref):
    jc = lax.broadcasted_iota(jnp.int32, (K_COLS, LANES), 0)
    lane = lax.broadcasted_iota(jnp.int32, (K_COLS, LANES), 1)
    c = lane & (Q_COLS - 1)
    neg = jnp.full((K_COLS, LANES), NEG_BIG, F32)
    for j, c0 in enumerate(COL_STARTS):
        wstart = jnp.clip(Q_COLS * j + c - WIN_COLS // 2, 0, GRID_W - WIN_COLS)
        col_ok = (c0 + jc >= wstart) & (c0 + jc < wstart + WIN_COLS)
        o = c0 - Q_COLS * j + WIN_COLS - 1
        in_row = [col_ok & (lane >= Q_COLS * r) & (lane < Q_COLS * (r + 1)) for r in range(Q_ROWS)]
        for h in range(HEADS_PER_BLOCK):
            rolled = {}
            for t, rel in enumerate(_WINDOW_PLAN):
                for jr in range(K_ROWS):
                    slab = neg
                    for r in range(Q_ROWS):
                        if not rel[r] <= jr < rel[r] + WIN_ROWS:
                            continue
                        dr = jr - ROW_TYPES[t] - r + WIN_ROWS - 1
                        assert 0 <= dr < RPB_ROWS
                        if (r, dr) not in rolled:
                            x = jnp.broadcast_to(w_ref[0, h, dr:dr + 1, :] * LOG2E, (K_COLS, LANES))
                            base = (Q_COLS * r + o - (RPB_COLS - 1)) % LANES
                            rolled[r, dr] = pltpu.roll(x, base, 1, stride=1, stride_axis=0)
                        slab = jnp.where(in_row[r], rolled[r, dr], slab)
                    tab_ref[h, t, j, pl.ds(jr * K_COLS, K_COLS), :] = slab


def _natten_kernel(q_ref, kc_ref, va_ref, vb_ref, w_ref, o_ref, tab_ref, *, rows, row_blocks):
    @pl.when((pl.program_id(1) == 0) & (pl.program_id(2) == 0))
    def _():
        _build_bias_table(w_ref, tab_ref)

    lane = lax.broadcasted_iota(jnp.int32, (UNIT_Q, LANES), 1)
    first = lane < HEAD_DIM
    vrefs = (va_ref, vb_ref)

    for rb in range(row_blocks):
        i0 = Q_ROWS * (pl.program_id(2) * row_blocks + rb)
        ks = jnp.clip(i0 - WIN_ROWS // 2, 0, rows - K_ROWS)
        rtype = lax.shift_right_logical(i0 - ks, 2)
        kstart = pl.multiple_of(ks * K_COLS, K_COLS)
        tok = slice(rb * Q_ROWS * GRID_W, (rb + 1) * Q_ROWS * GRID_W)
        qb = q_ref[0, tok, :].reshape(Q_ROWS, GRID_W, LANES)
        for j in range(N_COL_BLOCKS):
            cols = slice(j * Q_COLS, (j + 1) * Q_COLS)
            kw = kc_ref[0, j, pl.ds(kstart, UNIT_K), :]
            qp = qb[:, cols, :].reshape(UNIT_Q, LANES).astype(F32)
            outs = []
            for h in range(HEADS_PER_BLOCK):
                qm = jnp.where(first if h == 0 else jnp.logical_not(first), qp, 0.0).astype(BF16)
                s = lax.dot_general(kw, qm, (((1,), (1,)), ((), ())), preferred_element_type=F32)
                s = s + tab_ref[h, rtype, j]
                mx = jnp.max(s, axis=0, keepdims=True)
                pe = jnp.exp2(s - mx).astype(BF16)
                vw = vrefs[h][0, j, pl.ds(kstart, UNIT_K), :]
                outs.append(lax.dot_general(pe, vw, (((0,), (0,)), ((), ())), preferred_element_type=F32))
            num = jnp.where(first, outs[0], outs[1])
            den = pltpu.roll(jnp.where(first, outs[1], outs[0]), HEAD_DIM, 1)
            out = (num * (1.0 / den)).astype(BF16).reshape(Q_ROWS, Q_COLS, LANES)
            for r in range(Q_ROWS):
                o_ref[0, pl.ds(rb * Q_ROWS * GRID_W + r * GRID_W + j * Q_COLS, Q_COLS), :] = out[r]


def _outffn_kernel(x_ref, f_ref, a_ref, mod_ref, gf_ref, ga_ref, wf_ref, wo_ref, gffn_ref,
                   wg_ref, wu_ref, wd_ref, o_ref):
    x = x_ref[0]
    m = mod_ref[0]
    f = f_ref[0]
    fo = jnp.concatenate(
        [jnp.dot(f[:, g * FG_DIM:(g + 1) * FG_DIM], wf_ref[g], preferred_element_type=F32)
         for g in range(N_FGROUPS)], axis=-1)
    fn = _rms(fo) * gf_ref[...]
    an = _rms(a_ref[0].astype(F32)) * ga_ref[...]
    cat = jnp.concatenate([fn, an], axis=-1).astype(BF16)
    mix = jnp.dot(cat, wo_ref[...], preferred_element_type=F32)
    x1 = x + m[2:3] * mix
    h2 = (_rms(x1) * gffn_ref[...] * (1.0 + m[4:5]) + m[3:4]).astype(BF16)
    gate = jnp.dot(h2, wg_ref[...], preferred_element_type=F32)
    up = jnp.dot(h2, wu_ref[...], preferred_element_type=F32)
    act = (gate * jax.nn.sigmoid(gate) * up).astype(BF16)
    ff = jnp.dot(act, wd_ref[...], preferred_element_type=F32)
    o_ref[0] = x1 + m[5:6] * ff


def _bf16_const(a):
    return jnp.asarray(a, F32).astype(BF16)


def _const_spec(shape):
    nd = len(shape)
    return pl.BlockSpec(shape, lambda *_: (0,) * nd, pipeline_mode=pl.Buffered(1))


def _adaln(c_all, w_ada, b_ada):
    n = c_all.shape[0]
    tn = ADALN_TN
    return pl.pallas_call(
        _adaln_kernel,
        grid=(N_MOD * D_MODEL // tn,),
        in_specs=[pl.BlockSpec((n, D_MODEL), lambda j: (0, 0)),
                  pl.BlockSpec((D_MODEL, tn), lambda j: (0, j)),
                  pl.BlockSpec((1, tn), lambda j: (0, j))],
        out_specs=pl.BlockSpec((n, tn), lambda j: (0, j)),
        out_shape=jax.ShapeDtypeStruct((n, N_MOD * D_MODEL), F32),
        compiler_params=_cparams(1),
        name="adaln",
    )(c_all, w_ada, b_ada.reshape(1, -1))


def _inproj(x, mod, g_attn, w_in, pm, gq, gk, tm):
    b, t, _ = x.shape
    tok = lambda d: pl.BlockSpec((1, tm, d), lambda i, j: (i, j, 0))
    out = jax.ShapeDtypeStruct((b, t, D_FOURIER), BF16)
    kfrac = GRID_W // K_COLS
    colblk = pl.BlockSpec((1, N_COL_BLOCKS, tm // kfrac, D_NA), lambda i, j: (i, 0, j, 0))
    colblk_out = jax.ShapeDtypeStruct((b, N_COL_BLOCKS, t // kfrac, D_NA), BF16)
    return pl.pallas_call(
        _inproj_kernel,
        grid=(b, t // tm),
        in_specs=[tok(D_MODEL),
                  pl.BlockSpec((1, 8, D_MODEL), lambda i, j: (i, 0, 0)),
                  _const_spec((1, D_MODEL)), _const_spec((D_MODEL, D_IN)),
                  _const_spec((HEAD_MEAN_W, HEAD_MEAN_W)),
                  _const_spec((1, D_NA)), _const_spec((1, D_NA))],
        out_specs=[tok(D_FOURIER)] * 2 + [colblk] * 3,
        out_shape=[out] * 2 + [colblk_out] * 3,
        compiler_params=_cparams(2),
        name="inproj",
    )(x, mod, g_attn, w_in, pm, gq, gk)


def _fourier(u, cs, n1, n2):
    b, t, d = u.shape
    assert n1 * n2 == t and n2 % BF16_SUBLANES == 0 and n1 % BF16_SUBLANES == 0
    fa = _bf16_const(_stage_a_matrix(n1))
    gm = _bf16_const(_stage_b_matrices(n1, n2))
    f_block = k1_block = DFT_BLOCK
    n_a = n2 // f_block
    n_b = n1 // k1_block

    def u_index(i, j):
        stage1 = j < n_a
        return jnp.where(stage1, i, jnp.minimum(i + 1, b - 1)), 0, jnp.where(stage1, j, 0), 0

    in_spec = pl.BlockSpec((1, n1, f_block, d), u_index)
    out = pl.pallas_call(
        functools.partial(_dft_kernel, n_a_steps=n_a, f_block=f_block, k1_block=k1_block, n2=n2),
        grid=(b, n_a + n_b),
        in_specs=[in_spec, _const_spec((FG_DIM, 2 * FG_DIM)), _const_spec((2 * n1, 2 * n1)),
                  pl.BlockSpec((k1_block, n2, 2 * n2), lambda i, j: (jnp.maximum(j - n_a, 0), 0, 0))],
        out_specs=pl.BlockSpec((1, n2, k1_block, d), lambda i, j: (i, 0, jnp.maximum(j - n_a, 0), 0)),
        out_shape=jax.ShapeDtypeStruct((b, n2, n1, d), BF16),
        scratch_shapes=[pltpu.VMEM((2 * n1, n2, d), BF16)],
        compiler_params=_cparams(2),
        name="dft",
    )(u.reshape(b, n1, n2, d), cs, fa, gm)
    return out.reshape(b, t, d)


def _natten(q, kc, va, vb, bias_rows):
    b, t, _ = q.shape
    rows = t // GRID_W
    row_blocks = min(NATTEN_ROW_BLOCKS, rows // Q_ROWS)
    tq = row_blocks * Q_ROWS * GRID_W
    tk = kc.shape[2]
    kv_spec = pl.BlockSpec((1, N_COL_BLOCKS, tk, LANES), lambda hb, i, j: (i, 0, 0, hb))
    return pl.pallas_call(
        functools.partial(_natten_kernel, rows=rows, row_blocks=row_blocks),
        grid=(N_HEAD_BLOCKS, b, t // tq),
        in_specs=[pl.BlockSpec((1, tq, LANES), lambda hb, i, j: (i, j, hb)), kv_spec, kv_spec, kv_spec,
                  pl.BlockSpec((1, HEADS_PER_BLOCK, BIAS_ROWS, LANES), lambda hb, i, j: (hb, 0, 0, 0))],
        out_specs=pl.BlockSpec((1, tq, LANES), lambda hb, i, j: (i, j, hb)),
        out_shape=jax.ShapeDtypeStruct((b, t, D_NA), BF16),
        scratch_shapes=[pltpu.VMEM((HEADS_PER_BLOCK, len(ROW_TYPES), N_COL_BLOCKS, UNIT_K, UNIT_Q), F32)],
        compiler_params=_cparams(3),
        name="natten",
    )(q, kc, va, vb, bias_rows)


def _outffn(x, f, a, mod, gf, ga, wf, wo, gffn, wg, wu, wd, tm):
    b, t, _ = x.shape
    tok = lambda d: pl.BlockSpec((1, tm, d), lambda i, j: (i, j, 0))
    return pl.pallas_call(
        _outffn_kernel,
        grid=(b, t // tm),
        in_specs=[tok(D_MODEL), tok(D_FOURIER), tok(D_NA),
                  pl.BlockSpec((1, 8, D_MODEL), lambda i, j: (i, 0, 0)),
                  _const_spec((1, D_FOURIER)), _const_spec((1, D_NA)),
                  _const_spec((N_FGROUPS, FG_DIM, FG_DIM)), _const_spec((D_MODEL, D_MODEL)),
                  _const_spec((1, D_MODEL)), _const_spec((D_MODEL, D_FF)),
                  _const_spec((D_MODEL, D_FF)), _const_spec((D_FF, D_MODEL))],
        out_specs=tok(D_MODEL),
        out_shape=jax.ShapeDtypeStruct((b, t, D_MODEL), F32),
        compiler_params=_cparams(2),
        name="outffn",
    )(x, f, a, mod, gf, ga, wf, wo, gffn, wg, wu, wd)


def _stage_bias_rows(rpb):
    w = jnp.pad(rpb[:, :, ::-1], ((0, 0), (0, BIAS_ROWS - RPB_ROWS), (0, LANES - RPB_COLS)))
    return w.reshape(N_HEAD_BLOCKS, HEADS_PER_BLOCK, BIAS_ROWS, LANES)


def kernel(x_prompt, x_sample, c_prompt, c_sample, w_ada, b_ada, g_attn, w_in, g_q, g_k, w_fmix, rpb,
           g_fout, g_aout, w_o, g_ffn, w_gate, w_up, w_down):
    assert w_ada.shape[0] == 1
    nb_p, nb_s = c_prompt.shape[0], c_sample.shape[0]
    n_c = nb_p + nb_s
    c_all = jnp.concatenate([c_prompt, c_sample, jnp.zeros((-n_c % 8, D_MODEL), F32)], axis=0)
    mod = _adaln(c_all, w_ada[0], b_ada[0])[:n_c].reshape(n_c, N_MOD, D_MODEL)
    mod = jnp.pad(mod, ((0, 0), (0, 8 - N_MOD), (0, 0)))

    w_in_b = w_in[0].astype(BF16)
    w_o_b = w_o[0].astype(BF16)
    w_g_b = w_gate[0].astype(BF16)
    w_u_b = w_up[0].astype(BF16)
    w_d_b = w_down[0].astype(BF16)
    w_f_b = w_fmix[0].astype(BF16)
    cs = _bf16_const(_channel_dft_matrix())
    pm = _bf16_const(_head_mean_matrix())
    gq = jnp.tile(g_q[0], N_HEADS).reshape(1, D_NA)
    gk = jnp.tile(g_k[0], N_HEADS).reshape(1, D_NA)
    row = lambda v: v.reshape(1, -1)
    bias_rows = _stage_bias_rows(rpb[0])

    def trunk(x, mod_g):
        b, t, _ = x.shape
        u, q, kc, va, vb = _inproj(x, mod_g, row(g_attn[0]), w_in_b, pm, gq, gk, tm=INPROJ_TM)
        f = _fourier(u, cs, t // GRID_W, GRID_W)
        a = _natten(q, kc, va, vb, bias_rows)
        return _outffn(x, f, a, mod_g, row(g_fout[0]), row(g_aout[0]), w_f_b, w_o_b, row(g_ffn[0]),
                       w_g_b, w_u_b, w_d_b, tm=OUTFFN_TM)

    return trunk(x_prompt, mod[:nb_p]), trunk(x_sample, mod[nb_p:])
```

```python
import functools

import numpy as np
import jax
import jax.numpy as jnp
from jax import lax
from jax.experimental import pallas as pl
from jax.experimental.pallas import tpu as pltpu

F32 = jnp.float32
BF16 = jnp.bfloat16

D_MODEL = 1024
GRID_W = 64
D_FOURIER = 512
N_FGROUPS = 4
FG_DIM = 128
D_NA = 512
N_HEADS = 8
HEAD_DIM = 64
WIN_ROWS = 8
WIN_COLS = 16
RPB_ROWS = 2 * WIN_ROWS - 1
RPB_COLS = 2 * WIN_COLS - 1
D_IN = 2048
D_FF = 2816
N_MOD = 6
EPS = 1e-6

LANES = 128
HEAD_MEAN_W = 256
BF16_SUBLANES = 16
HEADS_PER_BLOCK = LANES // HEAD_DIM
N_HEAD_BLOCKS = N_HEADS // HEADS_PER_BLOCK
Q_ROWS = 8
Q_COLS = 16
UNIT_Q = Q_ROWS * Q_COLS
K_ROWS = 16
K_COLS = 32
UNIT_K = K_ROWS * K_COLS
N_COL_BLOCKS = GRID_W // Q_COLS
COL_STARTS = tuple(min(max(Q_COLS * j - WIN_COLS // 2, 0), GRID_W - K_COLS) for j in range(N_COL_BLOCKS))
ROW_TYPES = (0, WIN_ROWS // 2, WIN_ROWS)
BIAS_ROWS = RPB_ROWS + 1
NEG_BIG = -1e30
LOG2E = 1.4426950408889634

ADALN_TN = 1024
INPROJ_TM = 1024
OUTFFN_TM = 1024
DFT_F_BLOCK = 2 * BF16_SUBLANES
DFT_K1_BLOCK = 4 * BF16_SUBLANES
NATTEN_ROW_BLOCKS = 16
VMEM_LIMIT = 56 * 1024 * 1024


def _cparams(n_axes):
    return pltpu.CompilerParams(
        dimension_semantics=("arbitrary",) * n_axes, vmem_limit_bytes=VMEM_LIMIT)


def _channel_dft_matrix():
    c = np.arange(FG_DIM)
    ang = 2.0 * np.pi * ((c[:, None] * c[None, :]) % FG_DIM) / FG_DIM
    s = 1.0 / np.sqrt(FG_DIM)
    return np.concatenate([np.cos(ang) * s, np.sin(ang) * s], axis=1)


def _stage_a_matrix(n1):
    k = np.arange(n1)
    ang = 2.0 * np.pi * ((k[:, None] * k[None, :]) % n1) / n1
    c, s = np.cos(ang), np.sin(ang)
    re = np.concatenate([c, -s], axis=1)
    im = np.concatenate([-s, -c], axis=1)
    m = np.stack([re, im], axis=1).reshape(2 * n1, 2 * n1)
    return m / np.sqrt(n1)


def _stage_b_matrices(n1, n2):
    t = n1 * n2
    k1 = np.arange(n1)[:, None, None]
    k2 = np.arange(n2)[None, :, None]
    m = np.arange(n2)[None, None, :]
    idx = (m * k2 * n1 + m * k1) % t
    ang = 2.0 * np.pi * idx / t
    g = np.concatenate([np.cos(ang), np.sin(ang)], axis=2)
    return g / np.sqrt(n2)


def _head_mean_matrix():
    h = np.arange(HEAD_MEAN_W) // HEAD_DIM
    return (h[:, None] == h[None, :]).astype(np.float64) / HEAD_DIM


def _window_plan(rows):
    types = {}
    for blk in range(rows // Q_ROWS):
        i0 = Q_ROWS * blk
        ks = min(max(i0 - WIN_ROWS // 2, 0), rows - K_ROWS)
        rel = tuple(min(max(i0 + r - WIN_ROWS // 2, 0), rows - WIN_ROWS) - ks for r in range(Q_ROWS))
        assert all(0 <= x and x + WIN_ROWS <= K_ROWS for x in rel)
        assert types.setdefault(i0 - ks, rel) == rel
    assert sorted(types) == sorted(ROW_TYPES)
    return tuple(types[e] for e in ROW_TYPES)


_WINDOW_PLAN = _window_plan(64)
assert _WINDOW_PLAN == _window_plan(128)
for _j, _cs in enumerate(COL_STARTS):
    for _c in range(Q_COLS * _j, Q_COLS * (_j + 1)):
        _w0 = min(max(_c - WIN_COLS // 2, 0), GRID_W - WIN_COLS)
        assert _cs <= _w0 and _w0 + WIN_COLS <= _cs + K_COLS


def _split_bf16(x):
    hi = x.astype(BF16)
    return hi, (x - hi.astype(F32)).astype(BF16)


def _adaln_kernel(c_ref, w_ref, b_ref, o_ref):
    c = c_ref[...]
    s_hi, s_lo = _split_bf16(c * jax.nn.sigmoid(c))
    w_hi, w_lo = _split_bf16(w_ref[...])
    dot = functools.partial(jnp.dot, preferred_element_type=F32)
    o_ref[...] = dot(s_hi, w_hi) + (dot(s_hi, w_lo) + dot(s_lo, w_hi)) + b_ref[...]


def _rms(x):
    return x * lax.rsqrt(jnp.mean(x * x, axis=-1, keepdims=True) + EPS)


def _head_rms(t, p):
    sq = (t * t).astype(BF16)
    ms = jnp.concatenate(
        [jnp.dot(sq[:, c:c + HEAD_MEAN_W], p, preferred_element_type=F32) for c in range(0, D_NA, HEAD_MEAN_W)],
        axis=-1)
    return t * lax.rsqrt(ms + EPS)


def _inproj_kernel(x_ref, mod_ref, gattn_ref, win_ref, pm_ref, gq_ref, gk_ref,
                   u_ref, q_ref, kc_ref, va_ref, vb_ref):
    x = x_ref[0]
    m = mod_ref[0]
    h = (_rms(x) * gattn_ref[...] * (1.0 + m[1:2]) + m[0:1]).astype(BF16)
    z = jnp.dot(h, win_ref[...], preferred_element_type=F32)
    u_ref[0] = z[:, :D_FOURIER].astype(BF16)
    p = pm_ref[...]
    q = z[:, D_FOURIER:D_FOURIER + D_NA]
    k = z[:, D_FOURIER + D_NA:D_FOURIER + 2 * D_NA]
    q_ref[0] = (_head_rms(q, p) * gq_ref[...] * (HEAD_DIM ** -0.5 * LOG2E)).astype(BF16)
    kn = _head_rms(k, p) * gk_ref[...]
    v = z[:, D_FOURIER + 2 * D_NA:]
    even_head = (lax.broadcasted_iota(jnp.int32, v.shape, 1) & HEAD_DIM) == 0
    for src, dst in ((kn, kc_ref), (jnp.where(even_head, v, 1.0), va_ref), (jnp.where(even_head, 1.0, v), vb_ref)):
        grid = src.reshape(-1, GRID_W, D_NA)
        for j, c0 in enumerate(COL_STARTS):
            dst[0, j] = grid[:, c0:c0 + K_COLS, :].reshape(-1, D_NA).astype(BF16)


def _swap_leading(x):
    return jnp.swapaxes(x, 0, 1)


def _dft_kernel(u_ref, cs_ref, f_ref, g_ref, o_ref, y_ref, *, n_a_steps, f_block, k1_block, n2):
    j = pl.program_id(1)

    @pl.when(j < n_a_steps)
    def _():
        ut = _swap_leading(u_ref[0])
        n1 = ut.shape[1]
        fa = f_ref[...]
        cs = cs_ref[...]
        ab = [jnp.dot(ut[:, :, g * FG_DIM:(g + 1) * FG_DIM].reshape(f_block * n1, FG_DIM), cs,
                      preferred_element_type=F32).astype(BF16).reshape(f_block, n1, 2 * FG_DIM)
              for g in range(N_FGROUPS)]
        ys = []
        for f in range(f_block):
            x = jnp.concatenate([jnp.concatenate([ab[g][f, :, :FG_DIM] for g in range(N_FGROUPS)], axis=-1),
                                 jnp.concatenate([ab[g][f, :, FG_DIM:] for g in range(N_FGROUPS)], axis=-1)],
                                axis=0)
            ys.append(jnp.dot(fa, x, preferred_element_type=F32).astype(BF16))
        f0 = pl.multiple_of(j * f_block, f_block)
        y_ref[:, pl.ds(f0, f_block), :] = _swap_leading(jnp.stack(ys, axis=0))

    @pl.when(j >= n_a_steps)
    def _():
        r0 = pl.multiple_of((j - n_a_steps) * (2 * k1_block), 2 * k1_block)
        rs = []
        for i in range(k1_block):
            y = y_ref[pl.ds(r0 + 2 * i, 2)].reshape(2 * n2, D_FOURIER)
            rs.append(jnp.dot(g_ref[i], y, preferred_element_type=F32).astype(BF16))
        o_ref[0] = _swap_leading(jnp.stack(rs, axis=0))


def _build_bias_table(w_ref, tab_ref):
    jc = lax.broadcasted_iota(jnp.int32, (K_COLS, LANES), 0)
    lane = lax.broadcasted_iota(jnp.int32, (K_COLS, LANES), 1)
    c = lane & (Q_COLS - 1)
    neg = jnp.full((K_COLS, LANES), NEG_BIG, F32)
    for j, c0 in enumerate(COL_STARTS):
        wstart = jnp.clip(Q_COLS * j + c - WIN_COLS // 2, 0, GRID_W - WIN_COLS)
        col_ok = (c0 + jc >= wstart) & (c0 + jc < wstart + WIN_COLS)
        o = c0 - Q_COLS * j + WIN_COLS - 1
        in_row = [col_ok & (lane >= Q_COLS * r) & (lane < Q_COLS * (r + 1)) for r in range(Q_ROWS)]
        for h in range(HEADS_PER_BLOCK):
            rolled = {}
            for t, rel in enumerate(_WINDOW_PLAN):
                for jr in range(K_ROWS):
                    slab = neg
                    for r in range(Q_ROWS):
                        if not rel[r] <= jr < rel[r] + WIN_ROWS:
                            continue
                        dr = jr - ROW_TYPES[t] - r + WIN_ROWS - 1
                        assert 0 <= dr < RPB_ROWS
                        if (r, dr) not in rolled:
                            x = jnp.broadcast_to(w_ref[0, h, dr:dr + 1, :] * LOG2E, (K_COLS, LANES))
                            base = (Q_COLS * r + o - (RPB_COLS - 1)) % LANES
                            rolled[r, dr] = pltpu.roll(x, base, 1, stride=1, stride_axis=0)
                        slab = jnp.where(in_row[r], rolled[r, dr], slab)
                    tab_ref[h, t, j, pl.ds(jr * K_COLS, K_COLS), :] = slab


def _natten_kernel(q_ref, kc_ref, va_ref, vb_ref, w_ref, o_ref, tab_ref, *, rows, row_blocks):
    @pl.when((pl.program_id(1) == 0) & (pl.program_id(2) == 0))
    def _():
        _build_bias_table(w_ref, tab_ref)

    lane = lax.broadcasted_iota(jnp.int32, (UNIT_Q, LANES), 1)
    first = lane < HEAD_DIM
    vrefs = (va_ref, vb_ref)

    for rb in range(row_blocks):
        i0 = Q_ROWS * (pl.program_id(2) * row_blocks + rb)
        ks = jnp.clip(i0 - WIN_ROWS // 2, 0, rows - K_ROWS)
        rtype = lax.shift_right_logical(i0 - ks, 2)
        kstart = pl.multiple_of(ks * K_COLS, K_COLS)
        tok = slice(rb * Q_ROWS * GRID_W, (rb + 1) * Q_ROWS * GRID_W)
        qb = q_ref[0, tok, :].reshape(Q_ROWS, GRID_W, LANES)
        for j in range(N_COL_BLOCKS):
            cols = slice(j * Q_COLS, (j + 1) * Q_COLS)
            kw = kc_ref[0, j, pl.ds(kstart, UNIT_K), :]
            qp = qb[:, cols, :].reshape(UNIT_Q, LANES).astype(F32)
            outs = []
            for h in range(HEADS_PER_BLOCK):
                qm = jnp.where(first if h == 0 else jnp.logical_not(first), qp, 0.0).astype(BF16)
                s = lax.dot_general(kw, qm, (((1,), (1,)), ((), ())), preferred_element_type=F32)
                s = s + tab_ref[h, rtype, j]
                mx = jnp.max(s, axis=0, keepdims=True)
                pe = jnp.exp2(s - mx).astype(BF16)
                vw = vrefs[h][0, j, pl.ds(kstart, UNIT_K), :]
                outs.append(lax.dot_general(pe, vw, (((0,), (0,)), ((), ())), preferred_element_type=F32))
            num = jnp.where(first, outs[0], outs[1])
            den = pltpu.roll(jnp.where(first, outs[1], outs[0]), HEAD_DIM, 1)
            out = (num * (1.0 / den)).astype(BF16).reshape(Q_ROWS, Q_COLS, LANES)
            for r in range(Q_ROWS):
                o_ref[0, pl.ds(rb * Q_ROWS * GRID_W + r * GRID_W + j * Q_COLS, Q_COLS), :] = out[r]


def _outffn_kernel(x_ref, f_ref, a_ref, mod_ref, gf_ref, ga_ref, wf_ref, wo_ref, gffn_ref,
                   wg_ref, wu_ref, wd_ref, o_ref):
    x = x_ref[0]
    m = mod_ref[0]
    f = f_ref[0]
    fo = jnp.concatenate(
        [jnp.dot(f[:, g * FG_DIM:(g + 1) * FG_DIM], wf_ref[g], preferred_element_type=F32)
         for g in range(N_FGROUPS)], axis=-1)
    fn = _rms(fo) * gf_ref[...]
    an = _rms(a_ref[0].astype(F32)) * ga_ref[...]
    cat = jnp.concatenate([fn, an], axis=-1).astype(BF16)
    mix = jnp.dot(cat, wo_ref[...], preferred_element_type=F32)
    x1 = x + m[2:3] * mix
    h2 = (_rms(x1) * gffn_ref[...] * (1.0 + m[4:5]) + m[3:4]).astype(BF16)
    gate = jnp.dot(h2, wg_ref[...], preferred_element_type=F32)
    up = jnp.dot(h2, wu_ref[...], preferred_element_type=F32)
    act = (gate * jax.nn.sigmoid(gate) * up).astype(BF16)
    ff = jnp.dot(act, wd_ref[...], preferred_element_type=F32)
    o_ref[0] = x1 + m[5:6] * ff


def _bf16_const(a):
    return jnp.asarray(a, F32).astype(BF16)


def _const_spec(shape):
    nd = len(shape)
    return pl.BlockSpec(shape, lambda *_: (0,) * nd, pipeline_mode=pl.Buffered(1))


def _adaln(c_all, w_ada, b_ada):
    n = c_all.shape[0]
    tn = ADALN_TN
    return pl.pallas_call(
        _adaln_kernel,
        grid=(N_MOD * D_MODEL // tn,),
        in_specs=[pl.BlockSpec((n, D_MODEL), lambda j: (0, 0)),
                  pl.BlockSpec((D_MODEL, tn), lambda j: (0, j)),
                  pl.BlockSpec((1, tn), lambda j: (0, j))],
        out_specs=pl.BlockSpec((n, tn), lambda j: (0, j)),
        out_shape=jax.ShapeDtypeStruct((n, N_MOD * D_MODEL), F32),
        compiler_params=_cparams(1),
        name="adaln",
    )(c_all, w_ada, b_ada.reshape(1, -1))


def _inproj(x, mod, g_attn, w_in, pm, gq, gk, tm):
    b, t, _ = x.shape
    tok = lambda d: pl.BlockSpec((1, tm, d), lambda i, j: (i, j, 0))
    out = jax.ShapeDtypeStruct((b, t, D_FOURIER), BF16)
    kfrac = GRID_W // K_COLS
    colblk = pl.BlockSpec((1, N_COL_BLOCKS, tm // kfrac, D_NA), lambda i, j: (i, 0, j, 0))
    colblk_out = jax.ShapeDtypeStruct((b, N_COL_BLOCKS, t // kfrac, D_NA), BF16)
    return pl.pallas_call(
        _inproj_kernel,
        grid=(b, t // tm),
        in_specs=[tok(D_MODEL),
                  pl.BlockSpec((1, 8, D_MODEL), lambda i, j: (i, 0, 0)),
                  _const_spec((1, D_MODEL)), _const_spec((D_MODEL, D_IN)),
                  _const_spec((HEAD_MEAN_W, HEAD_MEAN_W)),
                  _const_spec((1, D_NA)), _const_spec((1, D_NA))],
        out_specs=[tok(D_FOURIER)] * 2 + [colblk] * 3,
        out_shape=[out] * 2 + [colblk_out] * 3,
        compiler_params=_cparams(2),
        name="inproj",
    )(x, mod, g_attn, w_in, pm, gq, gk)


def _fourier(u, cs, n1, n2):
    b, t, d = u.shape
    assert n1 * n2 == t and n2 % BF16_SUBLANES == 0 and n1 % BF16_SUBLANES == 0
    fa = _bf16_const(_stage_a_matrix(n1))
    gm = _bf16_const(_stage_b_matrices(n1, n2))
    f_block, k1_block = DFT_F_BLOCK, DFT_K1_BLOCK
    n_a = n2 // f_block
    n_b = n1 // k1_block

    def u_index(i, j):
        stage1 = j < n_a
        return jnp.where(stage1, i, jnp.minimum(i + 1, b - 1)), 0, jnp.where(stage1, j, 0), 0

    in_spec = pl.BlockSpec((1, n1, f_block, d), u_index)
    out = pl.pallas_call(
        functools.partial(_dft_kernel, n_a_steps=n_a, f_block=f_block, k1_block=k1_block, n2=n2),
        grid=(b, n_a + n_b),
        in_specs=[in_spec, _const_spec((FG_DIM, 2 * FG_DIM)), _const_spec((2 * n1, 2 * n1)),
                  pl.BlockSpec((k1_block, n2, 2 * n2), lambda i, j: (jnp.maximum(j - n_a, 0), 0, 0))],
        out_specs=pl.BlockSpec((1, n2, k1_block, d), lambda i, j: (i, 0, jnp.maximum(j - n_a, 0), 0)),
        out_shape=jax.ShapeDtypeStruct((b, n2, n1, d), BF16),
        scratch_shapes=[pltpu.VMEM((2 * n1, n2, d), BF16)],
        compiler_params=_cparams(2),
        name="dft",
    )(u.reshape(b, n1, n2, d), cs, fa, gm)
    return out.reshape(b, t, d)


def _natten(q, kc, va, vb, bias_rows):
    b, t, _ = q.shape
    rows = t // GRID_W
    row_blocks = min(NATTEN_ROW_BLOCKS, rows // Q_ROWS)
    tq = row_blocks * Q_ROWS * GRID_W
    tk = kc.shape[2]
    kv_spec = pl.BlockSpec((1, N_COL_BLOCKS, tk, LANES), lambda hb, i, j: (i, 0, 0, hb))
    return pl.pallas_call(
        functools.partial(_natten_kernel, rows=rows, row_blocks=row_blocks),
        grid=(N_HEAD_BLOCKS, b, t // tq),
        in_specs=[pl.BlockSpec((1, tq, LANES), lambda hb, i, j: (i, j, hb)), kv_spec, kv_spec, kv_spec,
                  pl.BlockSpec((1, HEADS_PER_BLOCK, BIAS_ROWS, LANES), lambda hb, i, j: (hb, 0, 0, 0))],
        out_specs=pl.BlockSpec((1, tq, LANES), lambda hb, i, j: (i, j, hb)),
        out_shape=jax.ShapeDtypeStruct((b, t, D_NA), BF16),
        scratch_shapes=[pltpu.VMEM((HEADS_PER_BLOCK, len(ROW_TYPES), N_COL_BLOCKS, UNIT_K, UNIT_Q), F32)],
        compiler_params=_cparams(3),
        name="natten",
    )(q, kc, va, vb, bias_rows)


def _outffn(x, f, a, mod, gf, ga, wf, wo, gffn, wg, wu, wd, tm):
    b, t, _ = x.shape
    tok = lambda d: pl.BlockSpec((1, tm, d), lambda i, j: (i, j, 0))
    return pl.pallas_call(
        _outffn_kernel,
        grid=(b, t // tm),
        in_specs=[tok(D_MODEL), tok(D_FOURIER), tok(D_NA),
                  pl.BlockSpec((1, 8, D_MODEL), lambda i, j: (i, 0, 0)),
                  _const_spec((1, D_FOURIER)), _const_spec((1, D_NA)),
                  _const_spec((N_FGROUPS, FG_DIM, FG_DIM)), _const_spec((D_MODEL, D_MODEL)),
                  _const_spec((1, D_MODEL)), _const_spec((D_MODEL, D_FF)),
                  _const_spec((D_MODEL, D_FF)), _const_spec((D_FF, D_MODEL))],
        out_specs=tok(D_MODEL),
        out_shape=jax.ShapeDtypeStruct((b, t, D_MODEL), F32),
        compiler_params=_cparams(2),
        name="outffn",
    )(x, f, a, mod, gf, ga, wf, wo, gffn, wg, wu, wd)


def _stage_bias_rows(rpb):
    w = jnp.pad(rpb[:, :, ::-1], ((0, 0), (0, BIAS_ROWS - RPB_ROWS), (0, LANES - RPB_COLS)))
    return w.reshape(N_HEAD_BLOCKS, HEADS_PER_BLOCK, BIAS_ROWS, LANES)


def kernel(x_prompt, x_sample, c_prompt, c_sample, w_ada, b_ada, g_attn, w_in, g_q, g_k, w_fmix, rpb,
           g_fout, g_aout, w_o, g_ffn, w_gate, w_up, w_down):
    assert w_ada.shape[0] == 1
    nb_p, nb_s = c_prompt.shape[0], c_sample.shape[0]
    n_c = nb_p + nb_s
    c_all = jnp.concatenate([c_prompt, c_sample, jnp.zeros((-n_c % 8, D_MODEL), F32)], axis=0)
    mod = _adaln(c_all, w_ada[0], b_ada[0])[:n_c].reshape(n_c, N_MOD, D_MODEL)
    mod = jnp.pad(mod, ((0, 0), (0, 8 - N_MOD), (0, 0)))

    w_in_b = w_in[0].astype(BF16)
    w_o_b = w_o[0].astype(BF16)
    w_g_b = w_gate[0].astype(BF16)
    w_u_b = w_up[0].astype(BF16)
    w_d_b = w_down[0].astype(BF16)
    w_f_b = w_fmix[0].astype(BF16)
    cs = _bf16_const(_channel_dft_matrix())
    pm = _bf16_const(_head_mean_matrix())
    gq = jnp.tile(g_q[0], N_HEADS).reshape(1, D_NA)
    gk = jnp.tile(g_k[0], N_HEADS).reshape(1, D_NA)
    row = lambda v: v.reshape(1, -1)
    bias_rows = _stage_bias_rows(rpb[0])

    def trunk(x, mod_g):
        b, t, _ = x.shape
        u, q, kc, va, vb = _inproj(x, mod_g, row(g_attn[0]), w_in_b, pm, gq, gk, tm=INPROJ_TM)
        f = _fourier(u, cs, t // GRID_W, GRID_W)
        a = _natten(q, kc, va, vb, bias_rows)
        return _outffn(x, f, a, mod_g, row(g_fout[0]), row(g_aout[0]), w_f_b, w_o_b, row(g_ffn[0]),
                       w_g_b, w_u_b, w_d_b, tm=OUTFFN_TM)

    return trunk(x_prompt, mod[:nb_p]), trunk(x_sample, mod[nb_p:])
```

```python
import functools

import numpy as np
import jax
import jax.numpy as jnp
from jax import lax
from jax.experimental import pallas as pl
from jax.experimental.pallas import tpu as pltpu

F32 = jnp.float32
BF16 = jnp.bfloat16

D_MODEL = 1024
GRID_W = 64
D_FOURIER = 512
N_FGROUPS = 4
FG_DIM = 128
D_NA = 512
N_HEADS = 8
HEAD_DIM = 64
WIN_ROWS = 8
WIN_COLS = 16
RPB_ROWS = 2 * WIN_ROWS - 1
RPB_COLS = 2 * WIN_COLS - 1
D_IN = 2048
D_FF = 2816
N_MOD = 6
EPS = 1e-6

LANES = 128
HEAD_MEAN_W = 256
BF16_SUBLANES = 16
HEADS_PER_BLOCK = LANES // HEAD_DIM
N_HEAD_BLOCKS = N_HEADS // HEADS_PER_BLOCK
Q_ROWS = 8
Q_COLS = 16
UNIT_Q = Q_ROWS * Q_COLS
K_ROWS = 16
K_COLS = 32
UNIT_K = K_ROWS * K_COLS
N_COL_BLOCKS = GRID_W // Q_COLS
COL_STARTS = tuple(min(max(Q_COLS * j - WIN_COLS // 2, 0), GRID_W - K_COLS) for j in range(N_COL_BLOCKS))
ROW_TYPES = (0, WIN_ROWS // 2, WIN_ROWS)
BIAS_ROWS = RPB_ROWS + 1
NEG_BIG = -1e30
LOG2E = 1.4426950408889634

ADALN_TN = 1024
INPROJ_TM = 1024
OUTFFN_TM = 1024
DFT_F_BLOCK = 2 * BF16_SUBLANES
DFT_K1_BLOCK = 4 * BF16_SUBLANES
NATTEN_ROW_BLOCKS = 16
VMEM_LIMIT = 56 * 1024 * 1024


def _cparams(n_axes):
    return pltpu.CompilerParams(
        dimension_semantics=("arbitrary",) * n_axes, vmem_limit_bytes=VMEM_LIMIT)


def _channel_dft_matrix():
    c = np.arange(FG_DIM)
    ang = 2.0 * np.pi * ((c[:, None] * c[None, :]) % FG_DIM) / FG_DIM
    s = 1.0 / np.sqrt(FG_DIM)
    return np.concatenate([np.cos(ang) * s, np.sin(ang) * s], axis=1)


def _stage_a_matrix(n1):
    k = np.arange(n1)
    ang = 2.0 * np.pi * ((k[:, None] * k[None, :]) % n1) / n1
    c, s = np.cos(ang), np.sin(ang)
    re = np.concatenate([c, -s], axis=1)
    im = np.concatenate([-s, -c], axis=1)
    m = np.stack([re, im], axis=1).reshape(2 * n1, 2 * n1)
    return m / np.sqrt(n1)


def _stage_b_matrices(n1, n2):
    t = n1 * n2
    k1 = np.arange(n1)[:, None, None]
    k2 = np.arange(n2)[None, :, None]
    m = np.arange(n2)[None, None, :]
    idx = (m * k2 * n1 + m * k1) % t
    ang = 2.0 * np.pi * idx / t
    g = np.concatenate([np.cos(ang), np.sin(ang)], axis=2)
    return g / np.sqrt(n2)


def _head_mean_matrix():
    h = np.arange(HEAD_MEAN_W) // HEAD_DIM
    return (h[:, None] == h[None, :]).astype(np.float64) / HEAD_DIM


def _window_plan(rows):
    types = {}
    for blk in range(rows // Q_ROWS):
        i0 = Q_ROWS * blk
        ks = min(max(i0 - WIN_ROWS // 2, 0), rows - K_ROWS)
        rel = tuple(min(max(i0 + r - WIN_ROWS // 2, 0), rows - WIN_ROWS) - ks for r in range(Q_ROWS))
        assert all(0 <= x and x + WIN_ROWS <= K_ROWS for x in rel)
        assert types.setdefault(i0 - ks, rel) == rel
    assert sorted(types) == sorted(ROW_TYPES)
    return tuple(types[e] for e in ROW_TYPES)


_WINDOW_PLAN = _window_plan(64)
assert _WINDOW_PLAN == _window_plan(128)
for _j, _cs in enumerate(COL_STARTS):
    for _c in range(Q_COLS * _j, Q_COLS * (_j + 1)):
        _w0 = min(max(_c - WIN_COLS // 2, 0), GRID_W - WIN_COLS)
        assert _cs <= _w0 and _w0 + WIN_COLS <= _cs + K_COLS


def _split_bf16(x):
    hi = x.astype(BF16)
    return hi, (x - hi.astype(F32)).astype(BF16)


def _adaln_kernel(c_ref, w_ref, b_ref, o_ref):
    c = c_ref[...]
    s_hi, s_lo = _split_bf16(c * jax.nn.sigmoid(c))
    w_hi, w_lo = _split_bf16(w_ref[...])
    dot = functools.partial(jnp.dot, preferred_element_type=F32)
    o_ref[...] = dot(s_hi, w_hi) + (dot(s_hi, w_lo) + dot(s_lo, w_hi)) + b_ref[...]


def _rms(x):
    return x * lax.rsqrt(jnp.mean(x * x, axis=-1, keepdims=True) + EPS)


def _head_rms(t, p):
    sq = (t * t).astype(BF16)
    ms = jnp.concatenate(
        [jnp.dot(sq[:, c:c + HEAD_MEAN_W], p, preferred_element_type=F32) for c in range(0, D_NA, HEAD_MEAN_W)],
        axis=-1)
    return t * lax.rsqrt(ms + EPS)


def _inproj_kernel(x_ref, mod_ref, gattn_ref, win_ref, pm_ref, gq_ref, gk_ref,
                   u_ref, q_ref, kc_ref, va_ref, vb_ref):
    x = x_ref[0]
    m = mod_ref[0]
    h = (_rms(x) * gattn_ref[...] * (1.0 + m[1:2]) + m[0:1]).astype(BF16)
    z = jnp.dot(h, win_ref[...], preferred_element_type=F32)
    u_ref[0] = z[:, :D_FOURIER].astype(BF16)
    p = pm_ref[...]
    q = z[:, D_FOURIER:D_FOURIER + D_NA]
    k = z[:, D_FOURIER + D_NA:D_FOURIER + 2 * D_NA]
    q_ref[0] = (_head_rms(q, p) * gq_ref[...] * (HEAD_DIM ** -0.5 * LOG2E)).astype(BF16)
    kn = _head_rms(k, p) * gk_ref[...]
    v = z[:, D_FOURIER + 2 * D_NA:]
    even_head = (lax.broadcasted_iota(jnp.int32, v.shape, 1) & HEAD_DIM) == 0
    for src, dst in ((kn, kc_ref), (jnp.where(even_head, v, 1.0), va_ref), (jnp.where(even_head, 1.0, v), vb_ref)):
        grid = src.reshape(-1, GRID_W, D_NA)
        for j, c0 in enumerate(COL_STARTS):
            dst[0, j] = grid[:, c0:c0 + K_COLS, :].reshape(-1, D_NA).astype(BF16)


def _swap_leading(x):
    return jnp.swapaxes(x, 0, 1)


def _dft_kernel(u_ref, cs_ref, f_ref, g_ref, o_ref, y_ref, *, n_a_steps, f_block, k1_block, n2):
    j = pl.program_id(1)

    @pl.when(j < n_a_steps)
    def _():
        ut = _swap_leading(u_ref[0])
        n1 = ut.shape[1]
        fa = f_ref[...]
        cs = cs_ref[...]
        ab = [jnp.dot(ut[:, :, g * FG_DIM:(g + 1) * FG_DIM].reshape(f_block * n1, FG_DIM), cs,
                      preferred_element_type=F32).astype(BF16).reshape(f_block, n1, 2 * FG_DIM)
              for g in range(N_FGROUPS)]
        ys = []
        for f in range(f_block):
            x = jnp.concatenate([jnp.concatenate([ab[g][f, :, :FG_DIM] for g in range(N_FGROUPS)], axis=-1),
                                 jnp.concatenate([ab[g][f, :, FG_DIM:] for g in range(N_FGROUPS)], axis=-1)],
                                axis=0)
            ys.append(jnp.dot(fa, x, preferred_element_type=F32).astype(BF16))
        f0 = pl.multiple_of(j * f_block, f_block)
        y_ref[:, pl.ds(f0, f_block), :] = _swap_leading(jnp.stack(ys, axis=0))

    @pl.when(j >= n_a_steps)
    def _():
        r0 = pl.multiple_of((j - n_a_steps) * (2 * k1_block), 2 * k1_block)
        rs = []
        for i in range(k1_block):
            y = y_ref[pl.ds(r0 + 2 * i, 2)].reshape(2 * n2, D_FOURIER)
            rs.append(jnp.dot(g_ref[i], y, preferred_element_type=F32).astype(BF16))
        o_ref[0] = _swap_leading(jnp.stack(rs, axis=0))


def _build_bias_table(w_ref, tab_ref):
    jc = lax.broadcasted_iota(jnp.int32, (K_COLS, LANES), 0)
    lane = lax.broadcasted_iota(jnp.int32, (K_COLS, LANES), 1)
    c = lane & (Q_COLS - 1)
    neg = jnp.full((K_COLS, LANES), NEG_BIG, F32)
    for j, c0 in enumerate(COL_STARTS):
        wstart = jnp.clip(Q_COLS * j + c - WIN_COLS // 2, 0, GRID_W - WIN_COLS)
        col_ok = (c0 + jc >= wstart) & (c0 + jc < wstart + WIN_COLS)
        o = c0 - Q_COLS * j + WIN_COLS - 1
        in_row = [col_ok & (lane >= Q_COLS * r) & (lane < Q_COLS * (r + 1)) for r in range(Q_ROWS)]
        for h in range(HEADS_PER_BLOCK):
            rolled = {}
            for t, rel in enumerate(_WINDOW_PLAN):
                for jr in range(K_ROWS):
                    slab = neg
                    for r in range(Q_ROWS):
                        if not rel[r] <= jr < rel[r] + WIN_ROWS:
                            continue
                        dr = jr - ROW_TYPES[t] - r + WIN_ROWS - 1
                        assert 0 <= dr < RPB_ROWS
                        if (r, dr) not in rolled:
                            x = jnp.broadcast_to(w_ref[0, h, dr:dr + 1, :] * LOG2E, (K_COLS, LANES))
                            base = (Q_COLS * r + o - (RPB_COLS - 1)) % LANES
                            rolled[r, dr] = pltpu.roll(x, base, 1, stride=1, stride_axis=0)
                        slab = jnp.where(in_row[r], rolled[r, dr], slab)
                    tab_ref[h, t, j, pl.ds(jr * K_COLS, K_COLS), :] = slab


def _natten_kernel(q_ref, kc_ref, va_ref, vb_ref, w_ref, o_ref, tab_ref, *, rows, row_blocks):
    @pl.when((pl.program_id(1) == 0) & (pl.program_id(2) == 0))
    def _():
        _build_bias_table(w_ref, tab_ref)

    lane = lax.broadcasted_iota(jnp.int32, (UNIT_Q, LANES), 1)
    first = lane < HEAD_DIM
    vrefs = (va_ref, vb_ref)

    for rb in range(row_blocks):
        i0 = Q_ROWS * (pl.program_id(2) * row_blocks + rb)
        ks = jnp.clip(i0 - WIN_ROWS // 2, 0, rows - K_ROWS)
        rtype = lax.shift_right_logical(i0 - ks, 2)
        kstart = pl.multiple_of(ks * K_COLS, K_COLS)
        tok = slice(rb * Q_ROWS * GRID_W, (rb + 1) * Q_ROWS * GRID_W)
        qb = q_ref[0, tok, :].reshape(Q_ROWS, GRID_W, LANES)
        for j in range(N_COL_BLOCKS):
            cols = slice(j * Q_COLS, (j + 1) * Q_COLS)
            kw = kc_ref[0, j, pl.ds(kstart, UNIT_K), :]
            qp = qb[:, cols, :].reshape(UNIT_Q, LANES)
            outs = []
            for h in range(HEADS_PER_BLOCK):
                qm = jnp.where(first if h == 0 else jnp.logical_not(first), qp, jnp.zeros_like(qp))
                s = lax.dot_general(kw, qm, (((1,), (1,)), ((), ())), preferred_element_type=F32)
                s = s + tab_ref[h, rtype, j]
                mx = jnp.max(s, axis=0, keepdims=True)
                pe = jnp.exp2(s - mx).astype(BF16)
                vw = vrefs[h][0, j, pl.ds(kstart, UNIT_K), :]
                outs.append(lax.dot_general(pe, vw, (((0,), (0,)), ((), ())), preferred_element_type=F32))
            num = jnp.where(first, outs[0], outs[1])
            den = pltpu.roll(jnp.where(first, outs[1], outs[0]), HEAD_DIM, 1)
            out = (num * (1.0 / den)).astype(BF16).reshape(Q_ROWS, Q_COLS, LANES)
            for r in range(Q_ROWS):
                o_ref[0, pl.ds(rb * Q_ROWS * GRID_W + r * GRID_W + j * Q_COLS, Q_COLS), :] = out[r]


def _outffn_kernel(x_ref, f_ref, a_ref, mod_ref, gf_ref, ga_ref, wf_ref, wo_ref, gffn_ref,
                   wg_ref, wu_ref, wd_ref, o_ref):
    x = x_ref[0]
    m = mod_ref[0]
    f = f_ref[0]
    fo = jnp.concatenate(
        [jnp.dot(f[:, g * FG_DIM:(g + 1) * FG_DIM], wf_ref[g], preferred_element_type=F32)
         for g in range(N_FGROUPS)], axis=-1)
    fn = _rms(fo) * gf_ref[...]
    an = _rms(a_ref[0].astype(F32)) * ga_ref[...]
    cat = jnp.concatenate([fn, an], axis=-1).astype(BF16)
    mix = jnp.dot(cat, wo_ref[...], preferred_element_type=F32)
    x1 = x + m[2:3] * mix
    h2 = (_rms(x1) * gffn_ref[...] * (1.0 + m[4:5]) + m[3:4]).astype(BF16)
    gate = jnp.dot(h2, wg_ref[...], preferred_element_type=F32)
    up = jnp.dot(h2, wu_ref[...], preferred_element_type=F32)
    act = (gate * jax.nn.sigmoid(gate) * up).astype(BF16)
    ff = jnp.dot(act, wd_ref[...], preferred_element_type=F32)
    o_ref[0] = x1 + m[5:6] * ff


def _bf16_const(a):
    return jnp.asarray(a, F32).astype(BF16)


def _const_spec(shape):
    nd = len(shape)
    return pl.BlockSpec(shape, lambda *_: (0,) * nd, pipeline_mode=pl.Buffered(1))


def _adaln(c_all, w_ada, b_ada):
    n = c_all.shape[0]
    tn = ADALN_TN
    return pl.pallas_call(
        _adaln_kernel,
        grid=(N_MOD * D_MODEL // tn,),
        in_specs=[pl.BlockSpec((n, D_MODEL), lambda j: (0, 0)),
                  pl.BlockSpec((D_MODEL, tn), lambda j: (0, j)),
                  pl.BlockSpec((1, tn), lambda j: (0, j))],
        out_specs=pl.BlockSpec((n, tn), lambda j: (0, j)),
        out_shape=jax.ShapeDtypeStruct((n, N_MOD * D_MODEL), F32),
        compiler_params=_cparams(1),
        name="adaln",
    )(c_all, w_ada, b_ada.reshape(1, -1))


def _inproj(x, mod, g_attn, w_in, pm, gq, gk, tm):
    b, t, _ = x.shape
    tok = lambda d: pl.BlockSpec((1, tm, d), lambda i, j: (i, j, 0))
    out = jax.ShapeDtypeStruct((b, t, D_FOURIER), BF16)
    kfrac = GRID_W // K_COLS
    colblk = pl.BlockSpec((1, N_COL_BLOCKS, tm // kfrac, D_NA), lambda i, j: (i, 0, j, 0))
    colblk_out = jax.ShapeDtypeStruct((b, N_COL_BLOCKS, t // kfrac, D_NA), BF16)
    return pl.pallas_call(
        _inproj_kernel,
        grid=(b, t // tm),
        in_specs=[tok(D_MODEL),
                  pl.BlockSpec((1, 8, D_MODEL), lambda i, j: (i, 0, 0)),
                  _const_spec((1, D_MODEL)), _const_spec((D_MODEL, D_IN)),
                  _const_spec((HEAD_MEAN_W, HEAD_MEAN_W)),
                  _const_spec((1, D_NA)), _const_spec((1, D_NA))],
        out_specs=[tok(D_FOURIER)] * 2 + [colblk] * 3,
        out_shape=[out] * 2 + [colblk_out] * 3,
        compiler_params=_cparams(2),
        name="inproj",
    )(x, mod, g_attn, w_in, pm, gq, gk)


def _fourier(u, cs, n1, n2):
    b, t, d = u.shape
    assert n1 * n2 == t and n2 % BF16_SUBLANES == 0 and n1 % BF16_SUBLANES == 0
    fa = _bf16_const(_stage_a_matrix(n1))
    gm = _bf16_const(_stage_b_matrices(n1, n2))
    f_block, k1_block = DFT_F_BLOCK, DFT_K1_BLOCK
    n_a = n2 // f_block
    n_b = n1 // k1_block

    def u_index(i, j):
        stage1 = j < n_a
        return jnp.where(stage1, i, jnp.minimum(i + 1, b - 1)), 0, jnp.where(stage1, j, 0), 0

    in_spec = pl.BlockSpec((1, n1, f_block, d), u_index)
    out = pl.pallas_call(
        functools.partial(_dft_kernel, n_a_steps=n_a, f_block=f_block, k1_block=k1_block, n2=n2),
        grid=(b, n_a + n_b),
        in_specs=[in_spec, _const_spec((FG_DIM, 2 * FG_DIM)), _const_spec((2 * n1, 2 * n1)),
                  pl.BlockSpec((k1_block, n2, 2 * n2), lambda i, j: (jnp.maximum(j - n_a, 0), 0, 0))],
        out_specs=pl.BlockSpec((1, n2, k1_block, d), lambda i, j: (i, 0, jnp.maximum(j - n_a, 0), 0)),
        out_shape=jax.ShapeDtypeStruct((b, n2, n1, d), BF16),
        scratch_shapes=[pltpu.VMEM((2 * n1, n2, d), BF16)],
        compiler_params=_cparams(2),
        name="dft",
    )(u.reshape(b, n1, n2, d), cs, fa, gm)
    return out.reshape(b, t, d)


def _natten(q, kc, va, vb, bias_rows):
    b, t, _ = q.shape
    rows = t // GRID_W
    row_blocks = min(NATTEN_ROW_BLOCKS, rows // Q_ROWS)
    tq = row_blocks * Q_ROWS * GRID_W
    tk = kc.shape[2]
    kv_spec = pl.BlockSpec((1, N_COL_BLOCKS, tk, LANES), lambda hb, i, j: (i, 0, 0, hb))
    return pl.pallas_call(
        functools.partial(_natten_kernel, rows=rows, row_blocks=row_blocks),
        grid=(N_HEAD_BLOCKS, b, t // tq),
        in_specs=[pl.BlockSpec((1, tq, LANES), lambda hb, i, j: (i, j, hb)), kv_spec, kv_spec, kv_spec,
                  pl.BlockSpec((1, HEADS_PER_BLOCK, BIAS_ROWS, LANES), lambda hb, i, j: (hb, 0, 0, 0))],
        out_specs=pl.BlockSpec((1, tq, LANES), lambda hb, i, j: (i, j, hb)),
        out_shape=jax.ShapeDtypeStruct((b, t, D_NA), BF16),
        scratch_shapes=[pltpu.VMEM((HEADS_PER_BLOCK, len(ROW_TYPES), N_COL_BLOCKS, UNIT_K, UNIT_Q), F32)],
        compiler_params=_cparams(3),
        name="natten",
    )(q, kc, va, vb, bias_rows)


def _outffn(x, f, a, mod, gf, ga, wf, wo, gffn, wg, wu, wd, tm):
    b, t, _ = x.shape
    tok = lambda d: pl.BlockSpec((1, tm, d), lambda i, j: (i, j, 0))
    return pl.pallas_call(
        _outffn_kernel,
        grid=(b, t // tm),
        in_specs=[tok(D_MODEL), tok(D_FOURIER), tok(D_NA),
                  pl.BlockSpec((1, 8, D_MODEL), lambda i, j: (i, 0, 0)),
                  _const_spec((1, D_FOURIER)), _const_spec((1, D_NA)),
                  _const_spec((N_FGROUPS, FG_DIM, FG_DIM)), _const_spec((D_MODEL, D_MODEL)),
                  _const_spec((1, D_MODEL)), _const_spec((D_MODEL, D_FF)),
                  _const_spec((D_MODEL, D_FF)), _const_spec((D_FF, D_MODEL))],
        out_specs=tok(D_MODEL),
        out_shape=jax.ShapeDtypeStruct((b, t, D_MODEL), F32),
        compiler_params=_cparams(2),
        name="outffn",
    )(x, f, a, mod, gf, ga, wf, wo, gffn, wg, wu, wd)


def _stage_bias_rows(rpb):
    w = jnp.pad(rpb[:, :, ::-1], ((0, 0), (0, BIAS_ROWS - RPB_ROWS), (0, LANES - RPB_COLS)))
    return w.reshape(N_HEAD_BLOCKS, HEADS_PER_BLOCK, BIAS_ROWS, LANES)


def kernel(x_prompt, x_sample, c_prompt, c_sample, w_ada, b_ada, g_attn, w_in, g_q, g_k, w_fmix, rpb,
           g_fout, g_aout, w_o, g_ffn, w_gate, w_up, w_down):
    assert w_ada.shape[0] == 1
    nb_p, nb_s = c_prompt.shape[0], c_sample.shape[0]
    n_c = nb_p + nb_s
    c_all = jnp.concatenate([c_prompt, c_sample, jnp.zeros((-n_c % 8, D_MODEL), F32)], axis=0)
    mod = _adaln(c_all, w_ada[0], b_ada[0])[:n_c].reshape(n_c, N_MOD, D_MODEL)
    mod = jnp.pad(mod, ((0, 0), (0, 8 - N_MOD), (0, 0)))

    w_in_b = w_in[0].astype(BF16)
    w_o_b = w_o[0].astype(BF16)
    w_g_b = w_gate[0].astype(BF16)
    w_u_b = w_up[0].astype(BF16)
    w_d_b = w_down[0].astype(BF16)
    w_f_b = w_fmix[0].astype(BF16)
    cs = _bf16_const(_channel_dft_matrix())
    pm = _bf16_const(_head_mean_matrix())
    gq = jnp.tile(g_q[0], N_HEADS).reshape(1, D_NA)
    gk = jnp.tile(g_k[0], N_HEADS).reshape(1, D_NA)
    row = lambda v: v.reshape(1, -1)
    bias_rows = _stage_bias_rows(rpb[0])

    def trunk(x, mod_g):
        b, t, _ = x.shape
        u, q, kc, va, vb = _inproj(x, mod_g, row(g_attn[0]), w_in_b, pm, gq, gk, tm=INPROJ_TM)
        f = _fourier(u, cs, t // GRID_W, GRID_W)
        a = _natten(q, kc, va, vb, bias_rows)
        return _outffn(x, f, a, mod_g, row(g_fout[0]), row(g_aout[0]), w_f_b, w_o_b, row(g_ffn[0]),
                       w_g_b, w_u_b, w_d_b, tm=OUTFFN_TM)

    return trunk(x_prompt, mod[:nb_p]), trunk(x_sample, mod[nb_p:])
```

```python
import functools

import numpy as np
import jax
import jax.numpy as jnp
from jax import lax
from jax.experimental import pallas as pl
from jax.experimental.pallas import tpu as pltpu

F32 = jnp.float32
BF16 = jnp.bfloat16

D_MODEL = 1024
GRID_W = 64
D_FOURIER = 512
N_FGROUPS = 4
FG_DIM = 128
D_NA = 512
N_HEADS = 8
HEAD_DIM = 64
WIN_ROWS = 8
WIN_COLS = 16
RPB_ROWS = 2 * WIN_ROWS - 1
RPB_COLS = 2 * WIN_COLS - 1
D_IN = 2048
D_FF = 2816
N_MOD = 6
EPS = 1e-6

LANES = 128
HEAD_MEAN_W = 256
BF16_SUBLANES = 16
HEADS_PER_BLOCK = LANES // HEAD_DIM
N_HEAD_BLOCKS = N_HEADS // HEADS_PER_BLOCK
Q_ROWS = 8
Q_COLS = 16
UNIT_Q = Q_ROWS * Q_COLS
K_ROWS = 16
K_COLS = 32
UNIT_K = K_ROWS * K_COLS
N_COL_BLOCKS = GRID_W // Q_COLS
COL_STARTS = tuple(min(max(Q_COLS * j - WIN_COLS // 2, 0), GRID_W - K_COLS) for j in range(N_COL_BLOCKS))
ROW_TYPES = (0, WIN_ROWS // 2, WIN_ROWS)
BIAS_ROWS = RPB_ROWS + 1
NEG_BIG = -1e30
LOG2E = 1.4426950408889634

ADALN_TN = 1024
INPROJ_TM = 1024
OUTFFN_TM = 1024
DFT_F_BLOCK = 2 * BF16_SUBLANES
DFT_K1_BLOCK = 4 * BF16_SUBLANES
NATTEN_ROW_BLOCKS = 16
VMEM_LIMIT = 56 * 1024 * 1024


def _cparams(n_axes):
    return pltpu.CompilerParams(
        dimension_semantics=("arbitrary",) * n_axes, vmem_limit_bytes=VMEM_LIMIT)


def _channel_dft_matrix():
    c = np.arange(FG_DIM)
    ang = 2.0 * np.pi * ((c[:, None] * c[None, :]) % FG_DIM) / FG_DIM
    s = 1.0 / np.sqrt(FG_DIM)
    return np.concatenate([np.cos(ang) * s, np.sin(ang) * s], axis=1)


def _stage_a_matrix(n1):
    k = np.arange(n1)
    ang = 2.0 * np.pi * ((k[:, None] * k[None, :]) % n1) / n1
    c, s = np.cos(ang), np.sin(ang)
    re = np.concatenate([c, -s], axis=1)
    im = np.concatenate([-s, -c], axis=1)
    m = np.stack([re, im], axis=1).reshape(2 * n1, 2 * n1)
    return m / np.sqrt(n1)


def _stage_b_matrices(n1, n2):
    t = n1 * n2
    k1 = np.arange(n1)[:, None, None]
    k2 = np.arange(n2)[None, :, None]
    m = np.arange(n2)[None, None, :]
    idx = (m * k2 * n1 + m * k1) % t
    ang = 2.0 * np.pi * idx / t
    g = np.concatenate([np.cos(ang), np.sin(ang)], axis=2)
    return g / np.sqrt(n2)


def _head_mean_matrix():
    h = np.arange(HEAD_MEAN_W) // HEAD_DIM
    return (h[:, None] == h[None, :]).astype(np.float64) / HEAD_DIM


def _window_plan(rows):
    types = {}
    for blk in range(rows // Q_ROWS):
        i0 = Q_ROWS * blk
        ks = min(max(i0 - WIN_ROWS // 2, 0), rows - K_ROWS)
        rel = tuple(min(max(i0 + r - WIN_ROWS // 2, 0), rows - WIN_ROWS) - ks for r in range(Q_ROWS))
        assert all(0 <= x and x + WIN_ROWS <= K_ROWS for x in rel)
        assert types.setdefault(i0 - ks, rel) == rel
    assert sorted(types) == sorted(ROW_TYPES)
    return tuple(types[e] for e in ROW_TYPES)


_WINDOW_PLAN = _window_plan(64)
assert _WINDOW_PLAN == _window_plan(128)
for _j, _cs in enumerate(COL_STARTS):
    for _c in range(Q_COLS * _j, Q_COLS * (_j + 1)):
        _w0 = min(max(_c - WIN_COLS // 2, 0), GRID_W - WIN_COLS)
        assert _cs <= _w0 and _w0 + WIN_COLS <= _cs + K_COLS


def _split_bf16(x):
    hi = x.astype(BF16)
    return hi, (x - hi.astype(F32)).astype(BF16)


def _adaln_kernel(c_ref, w_ref, b_ref, o_ref):
    c = c_ref[...]
    s_hi, s_lo = _split_bf16(c * jax.nn.sigmoid(c))
    w_hi, w_lo = _split_bf16(w_ref[...])
    dot = functools.partial(jnp.dot, preferred_element_type=F32)
    o_ref[...] = dot(s_hi, w_hi) + (dot(s_hi, w_lo) + dot(s_lo, w_hi)) + b_ref[...]


def _rms(x):
    return x * lax.rsqrt(jnp.mean(x * x, axis=-1, keepdims=True) + EPS)


def _head_rms(t, p):
    sq = (t * t).astype(BF16)
    ms = jnp.concatenate(
        [jnp.dot(sq[:, c:c + HEAD_MEAN_W], p, preferred_element_type=F32) for c in range(0, D_NA, HEAD_MEAN_W)],
        axis=-1)
    return t * lax.rsqrt(ms + EPS)


def _inproj_kernel(x_ref, mod_ref, gattn_ref, win_ref, pm_ref, gq_ref, gk_ref,
                   u_ref, q_ref, kc_ref, va_ref, vb_ref):
    x = x_ref[0]
    m = mod_ref[0]
    h = (_rms(x) * gattn_ref[...] * (1.0 + m[1:2]) + m[0:1]).astype(BF16)
    z = jnp.dot(h, win_ref[...], preferred_element_type=F32)
    u_ref[0] = z[:, :D_FOURIER].astype(BF16)
    p = pm_ref[...]
    q = z[:, D_FOURIER:D_FOURIER + D_NA]
    k = z[:, D_FOURIER + D_NA:D_FOURIER + 2 * D_NA]
    q_ref[0] = (_head_rms(q, p) * gq_ref[...] * (HEAD_DIM ** -0.5 * LOG2E)).astype(BF16)
    kn = _head_rms(k, p) * gk_ref[...]
    v = z[:, D_FOURIER + 2 * D_NA:]
    even_head = (lax.broadcasted_iota(jnp.int32, v.shape, 1) & HEAD_DIM) == 0
    for src, dst in ((kn, kc_ref), (jnp.where(even_head, v, 1.0), va_ref), (jnp.where(even_head, 1.0, v), vb_ref)):
        grid = src.reshape(-1, GRID_W, D_NA)
        for j, c0 in enumerate(COL_STARTS):
            dst[0, j] = grid[:, c0:c0 + K_COLS, :].reshape(-1, D_NA).astype(BF16)


def _swap_leading(x):
    return jnp.swapaxes(x, 0, 1)


def _dft_kernel(u_ref, cs_ref, f_ref, g_ref, o_ref, y_ref, *, n_a_steps, f_block, k1_block, n2):
    j = pl.program_id(1)

    @pl.when(j < n_a_steps)
    def _():
        ut = _swap_leading(u_ref[0])
        n1 = ut.shape[1]
        fa = f_ref[...]
        cs = cs_ref[...]
        ab = [jnp.dot(ut[:, :, g * FG_DIM:(g + 1) * FG_DIM].reshape(f_block * n1, FG_DIM), cs,
                      preferred_element_type=F32).astype(BF16).reshape(f_block, n1, 2 * FG_DIM)
              for g in range(N_FGROUPS)]
        ys = []
        for f in range(f_block):
            x = jnp.concatenate([jnp.concatenate([ab[g][f, :, :FG_DIM] for g in range(N_FGROUPS)], axis=-1),
                                 jnp.concatenate([ab[g][f, :, FG_DIM:] for g in range(N_FGROUPS)], axis=-1)],
                                axis=0)
            ys.append(jnp.dot(fa, x, preferred_element_type=F32).astype(BF16))
        f0 = pl.multiple_of(j * f_block, f_block)
        y_ref[:, pl.ds(f0, f_block), :] = _swap_leading(jnp.stack(ys, axis=0))

    @pl.when(j >= n_a_steps)
    def _():
        k0 = (j - n_a_steps) * k1_block
        r0 = pl.multiple_of(2 * k0, 2 * k1_block)
        rs = []
        for i in range(k1_block):
            y = y_ref[pl.ds(r0 + 2 * i, 2)].reshape(2 * n2, D_FOURIER)
            rs.append(jnp.dot(g_ref[k0 + i], y, preferred_element_type=F32).astype(BF16))
        o_ref[0] = _swap_leading(jnp.stack(rs, axis=0))


def _build_bias_table(w_ref, tab_ref):
    jc = lax.broadcasted_iota(jnp.int32, (K_COLS, LANES), 0)
    lane = lax.broadcasted_iota(jnp.int32, (K_COLS, LANES), 1)
    c = lane & (Q_COLS - 1)
    neg = jnp.full((K_COLS, LANES), NEG_BIG, F32)
    for j, c0 in enumerate(COL_STARTS):
        wstart = jnp.clip(Q_COLS * j + c - WIN_COLS // 2, 0, GRID_W - WIN_COLS)
        col_ok = (c0 + jc >= wstart) & (c0 + jc < wstart + WIN_COLS)
        o = c0 - Q_COLS * j + WIN_COLS - 1
        in_row = [col_ok & (lane >= Q_COLS * r) & (lane < Q_COLS * (r + 1)) for r in range(Q_ROWS)]
        for h in range(HEADS_PER_BLOCK):
            rolled = {}
            for t, rel in enumerate(_WINDOW_PLAN):
                for jr in range(K_ROWS):
                    slab = neg
                    for r in range(Q_ROWS):
                        if not rel[r] <= jr < rel[r] + WIN_ROWS:
                            continue
                        dr = jr - ROW_TYPES[t] - r + WIN_ROWS - 1
                        assert 0 <= dr < RPB_ROWS
                        if (r, dr) not in rolled:
                            x = jnp.broadcast_to(w_ref[0, h, dr:dr + 1, :] * LOG2E, (K_COLS, LANES))
                            base = (Q_COLS * r + o - (RPB_COLS - 1)) % LANES
                            rolled[r, dr] = pltpu.roll(x, base, 1, stride=1, stride_axis=0)
                        slab = jnp.where(in_row[r], rolled[r, dr], slab)
                    tab_ref[h, t, j, pl.ds(jr * K_COLS, K_COLS), :] = slab


def _natten_kernel(q_ref, kc_ref, va_ref, vb_ref, w_ref, o_ref, tab_ref, *, rows, row_blocks):
    @pl.when((pl.program_id(1) == 0) & (pl.program_id(2) == 0))
    def _():
        _build_bias_table(w_ref, tab_ref)

    lane = lax.broadcasted_iota(jnp.int32, (UNIT_Q, LANES), 1)
    first = lane < HEAD_DIM
    vrefs = (va_ref, vb_ref)

    for rb in range(row_blocks):
        i0 = Q_ROWS * (pl.program_id(2) * row_blocks + rb)
        ks = jnp.clip(i0 - WIN_ROWS // 2, 0, rows - K_ROWS)
        rtype = lax.shift_right_logical(i0 - ks, 2)
        kstart = pl.multiple_of(ks * K_COLS, K_COLS)
        tok = slice(rb * Q_ROWS * GRID_W, (rb + 1) * Q_ROWS * GRID_W)
        qb = q_ref[0, tok, :].reshape(Q_ROWS, GRID_W, LANES)
        for j in range(N_COL_BLOCKS):
            cols = slice(j * Q_COLS, (j + 1) * Q_COLS)
            kw = kc_ref[0, j, pl.ds(kstart, UNIT_K), :]
            qp = qb[:, cols, :].reshape(UNIT_Q, LANES)
            outs = []
            for h in range(HEADS_PER_BLOCK):
                qm = jnp.where(first if h == 0 else jnp.logical_not(first), qp, jnp.zeros_like(qp))
                s = lax.dot_general(kw, qm, (((1,), (1,)), ((), ())), preferred_element_type=F32)
                s = s + tab_ref[h, rtype, j]
                mx = jnp.max(s, axis=0, keepdims=True)
                pe = jnp.exp2(s - mx).astype(BF16)
                vw = vrefs[h][0, j, pl.ds(kstart, UNIT_K), :]
                outs.append(lax.dot_general(pe, vw, (((0,), (0,)), ((), ())), preferred_element_type=F32))
            num = jnp.where(first, outs[0], outs[1])
            den = pltpu.roll(jnp.where(first, outs[1], outs[0]), HEAD_DIM, 1)
            out = (num * (1.0 / den)).astype(BF16).reshape(Q_ROWS, Q_COLS, LANES)
            for r in range(Q_ROWS):
                o_ref[0, pl.ds(rb * Q_ROWS * GRID_W + r * GRID_W + j * Q_COLS, Q_COLS), :] = out[r]


def _outffn_kernel(x_ref, f_ref, a_ref, mod_ref, gf_ref, ga_ref, wf_ref, wo_ref, gffn_ref,
                   wg_ref, wu_ref, wd_ref, o_ref):
    x = x_ref[0]
    m = mod_ref[0]
    f = f_ref[0]
    fo = jnp.concatenate(
        [jnp.dot(f[:, g * FG_DIM:(g + 1) * FG_DIM], wf_ref[g], preferred_element_type=F32)
         for g in range(N_FGROUPS)], axis=-1)
    fn = _rms(fo) * gf_ref[...]
    an = _rms(a_ref[0].astype(F32)) * ga_ref[...]
    cat = jnp.concatenate([fn, an], axis=-1).astype(BF16)
    mix = jnp.dot(cat, wo_ref[...], preferred_element_type=F32)
    x1 = x + m[2:3] * mix
    h2 = (_rms(x1) * gffn_ref[...] * (1.0 + m[4:5]) + m[3:4]).astype(BF16)
    gate = jnp.dot(h2, wg_ref[...], preferred_element_type=F32)
    up = jnp.dot(h2, wu_ref[...], preferred_element_type=F32)
    act = (gate * jax.nn.sigmoid(gate) * up).astype(BF16)
    ff = jnp.dot(act, wd_ref[...], preferred_element_type=F32)
    o_ref[0] = x1 + m[5:6] * ff


def _bf16_const(a):
    return jnp.asarray(a, F32).astype(BF16)


def _const_spec(shape):
    nd = len(shape)
    return pl.BlockSpec(shape, lambda *_: (0,) * nd, pipeline_mode=pl.Buffered(1))


def _adaln(c_all, w_ada, b_ada):
    n = c_all.shape[0]
    tn = ADALN_TN
    return pl.pallas_call(
        _adaln_kernel,
        grid=(N_MOD * D_MODEL // tn,),
        in_specs=[pl.BlockSpec((n, D_MODEL), lambda j: (0, 0)),
                  pl.BlockSpec((D_MODEL, tn), lambda j: (0, j)),
                  pl.BlockSpec((1, tn), lambda j: (0, j))],
        out_specs=pl.BlockSpec((n, tn), lambda j: (0, j)),
        out_shape=jax.ShapeDtypeStruct((n, N_MOD * D_MODEL), F32),
        compiler_params=_cparams(1),
        name="adaln",
    )(c_all, w_ada, b_ada.reshape(1, -1))


def _inproj(x, mod, g_attn, w_in, pm, gq, gk, tm):
    b, t, _ = x.shape
    tok = lambda d: pl.BlockSpec((1, tm, d), lambda i, j: (i, j, 0))
    out = jax.ShapeDtypeStruct((b, t, D_FOURIER), BF16)
    kfrac = GRID_W // K_COLS
    colblk = pl.BlockSpec((1, N_COL_BLOCKS, tm // kfrac, D_NA), lambda i, j: (i, 0, j, 0))
    colblk_out = jax.ShapeDtypeStruct((b, N_COL_BLOCKS, t // kfrac, D_NA), BF16)
    return pl.pallas_call(
        _inproj_kernel,
        grid=(b, t // tm),
        in_specs=[tok(D_MODEL),
                  pl.BlockSpec((1, 8, D_MODEL), lambda i, j: (i, 0, 0)),
                  _const_spec((1, D_MODEL)), _const_spec((D_MODEL, D_IN)),
                  _const_spec((HEAD_MEAN_W, HEAD_MEAN_W)),
                  _const_spec((1, D_NA)), _const_spec((1, D_NA))],
        out_specs=[tok(D_FOURIER)] * 2 + [colblk] * 3,
        out_shape=[out] * 2 + [colblk_out] * 3,
        compiler_params=_cparams(2),
        name="inproj",
    )(x, mod, g_attn, w_in, pm, gq, gk)


def _fourier(u, cs, n1, n2):
    b, t, d = u.shape
    assert n1 * n2 == t and n2 % BF16_SUBLANES == 0 and n1 % BF16_SUBLANES == 0
    fa = _bf16_const(_stage_a_matrix(n1))
    gm = _bf16_const(_stage_b_matrices(n1, n2))
    f_block, k1_block = DFT_F_BLOCK, DFT_K1_BLOCK
    n_a = n2 // f_block
    n_b = n1 // k1_block

    def u_index(i, j):
        stage1 = j < n_a
        return jnp.where(stage1, i, jnp.minimum(i + 1, b - 1)), 0, jnp.where(stage1, j, 0), 0

    in_spec = pl.BlockSpec((1, n1, f_block, d), u_index)
    out = pl.pallas_call(
        functools.partial(_dft_kernel, n_a_steps=n_a, f_block=f_block, k1_block=k1_block, n2=n2),
        grid=(b, n_a + n_b),
        in_specs=[in_spec, _const_spec((FG_DIM, 2 * FG_DIM)), _const_spec((2 * n1, 2 * n1)),
                  _const_spec((n1, n2, 2 * n2))],
        out_specs=pl.BlockSpec((1, n2, k1_block, d), lambda i, j: (i, 0, jnp.maximum(j - n_a, 0), 0)),
        out_shape=jax.ShapeDtypeStruct((b, n2, n1, d), BF16),
        scratch_shapes=[pltpu.VMEM((2 * n1, n2, d), BF16)],
        compiler_params=_cparams(2),
        name="dft",
    )(u.reshape(b, n1, n2, d), cs, fa, gm)
    return out.reshape(b, t, d)


def _natten(q, kc, va, vb, bias_rows):
    b, t, _ = q.shape
    rows = t // GRID_W
    row_blocks = min(NATTEN_ROW_BLOCKS, rows // Q_ROWS)
    tq = row_blocks * Q_ROWS * GRID_W
    tk = kc.shape[2]
    kv_spec = pl.BlockSpec((1, N_COL_BLOCKS, tk, LANES), lambda hb, i, j: (i, 0, 0, hb))
    return pl.pallas_call(
        functools.partial(_natten_kernel, rows=rows, row_blocks=row_blocks),
        grid=(N_HEAD_BLOCKS, b, t // tq),
        in_specs=[pl.BlockSpec((1, tq, LANES), lambda hb, i, j: (i, j, hb)), kv_spec, kv_spec, kv_spec,
                  pl.BlockSpec((1, HEADS_PER_BLOCK, BIAS_ROWS, LANES), lambda hb, i, j: (hb, 0, 0, 0))],
        out_specs=pl.BlockSpec((1, tq, LANES), lambda hb, i, j: (i, j, hb)),
        out_shape=jax.ShapeDtypeStruct((b, t, D_NA), BF16),
        scratch_shapes=[pltpu.VMEM((HEADS_PER_BLOCK, len(ROW_TYPES), N_COL_BLOCKS, UNIT_K, UNIT_Q), F32)],
        compiler_params=_cparams(3),
        name="natten",
    )(q, kc, va, vb, bias_rows)


def _outffn(x, f, a, mod, gf, ga, wf, wo, gffn, wg, wu, wd, tm):
    b, t, _ = x.shape
    tok = lambda d: pl.BlockSpec((1, tm, d), lambda i, j: (i, j, 0))
    return pl.pallas_call(
        _outffn_kernel,
        grid=(b, t // tm),
        in_specs=[tok(D_MODEL), tok(D_FOURIER), tok(D_NA),
                  pl.BlockSpec((1, 8, D_MODEL), lambda i, j: (i, 0, 0)),
                  _const_spec((1, D_FOURIER)), _const_spec((1, D_NA)),
                  _const_spec((N_FGROUPS, FG_DIM, FG_DIM)), _const_spec((D_MODEL, D_MODEL)),
                  _const_spec((1, D_MODEL)), _const_spec((D_MODEL, D_FF)),
                  _const_spec((D_MODEL, D_FF)), _const_spec((D_FF, D_MODEL))],
        out_specs=tok(D_MODEL),
        out_shape=jax.ShapeDtypeStruct((b, t, D_MODEL), F32),
        compiler_params=_cparams(2),
        name="outffn",
    )(x, f, a, mod, gf, ga, wf, wo, gffn, wg, wu, wd)


def _stage_bias_rows(rpb):
    w = jnp.pad(rpb[:, :, ::-1], ((0, 0), (0, BIAS_ROWS - RPB_ROWS), (0, LANES - RPB_COLS)))
    return w.reshape(N_HEAD_BLOCKS, HEADS_PER_BLOCK, BIAS_ROWS, LANES)


def kernel(x_prompt, x_sample, c_prompt, c_sample, w_ada, b_ada, g_attn, w_in, g_q, g_k, w_fmix, rpb,
           g_fout, g_aout, w_o, g_ffn, w_gate, w_up, w_down):
    assert w_ada.shape[0] == 1
    nb_p, nb_s = c_prompt.shape[0], c_sample.shape[0]
    n_c = nb_p + nb_s
    c_all = jnp.concatenate([c_prompt, c_sample, jnp.zeros((-n_c % 8, D_MODEL), F32)], axis=0)
    mod = _adaln(c_all, w_ada[0], b_ada[0])[:n_c].reshape(n_c, N_MOD, D_MODEL)
    mod = jnp.pad(mod, ((0, 0), (0, 8 - N_MOD), (0, 0)))

    w_in_b = w_in[0].astype(BF16)
    w_o_b = w_o[0].astype(BF16)
    w_g_b = w_gate[0].astype(BF16)
    w_u_b = w_up[0].astype(BF16)
    w_d_b = w_down[0].astype(BF16)
    w_f_b = w_fmix[0].astype(BF16)
    cs = _bf16_const(_channel_dft_matrix())
    pm = _bf16_const(_head_mean_matrix())
    gq = jnp.tile(g_q[0], N_HEADS).reshape(1, D_NA)
    gk = jnp.tile(g_k[0], N_HEADS).reshape(1, D_NA)
    row = lambda v: v.reshape(1, -1)
    bias_rows = _stage_bias_rows(rpb[0])

    def trunk(x, mod_g):
        b, t, _ = x.shape
        u, q, kc, va, vb = _inproj(x, mod_g, row(g_attn[0]), w_in_b, pm, gq, gk, tm=INPROJ_TM)
        f = _fourier(u, cs, t // GRID_W, GRID_W)
        a = _natten(q, kc, va, vb, bias_rows)
        return _outffn(x, f, a, mod_g, row(g_fout[0]), row(g_aout[0]), w_f_b, w_o_b, row(g_ffn[0]),
                       w_g_b, w_u_b, w_d_b, tm=OUTFFN_TM)

    return trunk(x_prompt, mod[:nb_p]), trunk(x_sample, mod[nb_p:])
```
